```python
import math
import jax, jax.numpy as jnp
from jax import lax
import numpy as np

D_MODEL = 1024
BATCH = 8
SEQ = 2048
DEPTH = 2
DEC_BATCH = 32
DEC_SEQ = 1
PAST_LEN = 16384
PAGE_SIZE = 128

N_EVEN = (DEPTH + 1) // 2
N_ODD = DEPTH // 2
RNN_WIDTH = D_MODEL // 2
RNN_HEADS = 8
RNN_HEAD_DIM = RNN_WIDTH // RNN_HEADS
CONV_W = 4
LRU_C = 8.0
SGU_WIDTH = D_MODEL // 2
SGU_GROUPS = 4
SGU_GROUP_DIM = SGU_WIDTH // SGU_GROUPS
CHUNK = 128
MIX_WIDTH = RNN_WIDTH + SGU_WIDTH
IN0_WIDTH = 2 * RNN_WIDTH + 2 * SGU_WIDTH
N_HEADS = 8
HEAD_DIM = D_MODEL // (2 * N_HEADS)
V_DIM = 2 * HEAD_DIM
ROT_DIM = HEAD_DIM // 4
ROPE_THETA = 500000.0
Q_BLOCK = 128
D_FF = 2816
N_EXPERTS = 8
TOP_K = 2
D_FF_EXPERT = 3584
LN_EPS = 1e-5
DEEPNORM_ALPHA = (2 * DEPTH) ** 0.25
DEEPNORM_BETA = (8 * DEPTH) ** -0.25

kernel_name = "hybrid_rglru_sgu_diffattn_moe_decode_step"


def _layernorm(x, g, b):
    xf = x.astype(jnp.float32)
    mu = jnp.mean(xf, -1, keepdims=True)
    var = jnp.mean(jnp.square(xf - mu), -1, keepdims=True)
    return ((xf - mu) * lax.rsqrt(var + LN_EPS) * g + b).astype(x.dtype)


def _rmsnorm(x, g):
    xf = x.astype(jnp.float32)
    return (xf * lax.rsqrt(jnp.mean(xf * xf, -1, keepdims=True) + LN_EPS) * g).astype(x.dtype)


def _modulate(x, c, w_mod, b_mod):
    m = (jax.nn.silu(c) @ w_mod + b_mod)[:, None, :]
    shift, scale, gate = jnp.split(m, 3, axis=-1)
    return x * (1 + scale) + shift, 1 + gate


def _post_norm(x, out, gate, g, b):
    return _layernorm(DEEPNORM_ALPHA * x + gate * out, g, b)


def _causal_conv(x, buf, w, b):
    S = x.shape[1]
    xp = jnp.concatenate([buf.astype(x.dtype), x], axis=1)
    y = sum(xp[:, j:j + S] * w[j] for j in range(CONV_W)) + b
    return y, xp[:, xp.shape[1] - (CONV_W - 1):]


def _block_diag(x, w, b):
    xs = x.reshape(x.shape[:-1] + (RNN_HEADS, RNN_HEAD_DIM))
    return jnp.einsum('bshi,hij->bshj', xs, w).reshape(x.shape) + b


def _rg_lru(x, h0, w_a, b_a, w_x, b_x, lam):
    xf = x.astype(jnp.float32)
    r = jax.nn.sigmoid(_block_diag(x, w_a, b_a).astype(jnp.float32))
    i = jax.nn.sigmoid(_block_diag(x, w_x, b_x).astype(jnp.float32))
    log_a = -LRU_C * r * jax.nn.softplus(-lam.astype(jnp.float32))
    a = jnp.exp(log_a)
    u = jnp.sqrt(-jnp.expm1(2.0 * log_a)) * (i * xf)

    def step(h, au):
        a_t, u_t = au
        h = a_t * h + u_t
        return h, h

    hT, hs = lax.scan(step, h0.astype(jnp.float32), (jnp.swapaxes(a, 0, 1), jnp.swapaxes(u, 0, 1)))
    return jnp.swapaxes(hs, 0, 1).astype(x.dtype), hT.astype(h0.dtype)


def _spatial_gate(u, v, ln_g, ln_b, w_s, b_s):
    B, S, _ = v.shape
    vn = _layernorm(v.reshape(B, S, SGU_GROUPS, SGU_GROUP_DIM), ln_g, ln_b)
    pad = (-S) % CHUNK
    nc = (S + pad) // CHUNK
    vc = jnp.pad(vn, ((0, 0), (0, pad), (0, 0), (0, 0))).reshape(B, nc, CHUNK, SGU_GROUPS, SGU_GROUP_DIM)
    ws = jnp.where(jnp.tril(jnp.ones((CHUNK, CHUNK), bool)), w_s, 0)
    s = jnp.einsum('gts,bnsgd->bntgd', ws, vc) + b_s.T[None, None, :, :, None]
    s = s.reshape(B, nc * CHUNK, SGU_GROUPS, SGU_GROUP_DIM)[:, :S].reshape(B, S, SGU_WIDTH)
    return u * s, vn.reshape(B, S, SGU_WIDTH)


def _rglru_sgu_mixer(h, conv_buf, h0, w_in, conv_w, conv_b, wa, ba, wx, bx, lam, ln_g, ln_b, ws, bs, w_out):
    z = h @ w_in
    xr, gr, u, v = jnp.split(z, [RNN_WIDTH, 2 * RNN_WIDTH, 2 * RNN_WIDTH + SGU_WIDTH], axis=-1)
    xc, new_buf = _causal_conv(xr, conv_buf, conv_w, conv_b)
    hs, hT = _rg_lru(xc, h0, wa, ba, wx, bx, lam)
    rnn_out = hs * jax.nn.gelu(gr)
    sgu_out, vn = _spatial_gate(jax.nn.gelu(u), jax.nn.gelu(v), ln_g, ln_b, ws, bs)
    out = jnp.concatenate([rnn_out, sgu_out], axis=-1) @ w_out
    return out, new_buf, hT, vn


def _rope(x, pos):
    half = ROT_DIM // 2
    inv = ROPE_THETA ** (-(jnp.arange(half, dtype=jnp.float32) * 2.0 / ROT_DIM))
    ang = pos.astype(jnp.float32)[:, None] * inv
    cos = jnp.cos(ang)[None, :, None, None, :]
    sin = jnp.sin(ang)[None, :, None, None, :]
    xf = x.astype(jnp.float32)
    x1, x2 = xf[..., :half], xf[..., half:ROT_DIM]
    return jnp.concatenate([x1 * cos - x2 * sin, x2 * cos + x1 * sin, xf[..., ROT_DIM:]], -1).astype(x.dtype)


def _diff_qkv(h, pos, w_qkv):
    B, S, _ = h.shape
    q, k, v = jnp.split(h @ w_qkv, 3, axis=-1)
    q = _rope(q.reshape(B, S, N_HEADS, 2, HEAD_DIM), pos)
    k = _rope(k.reshape(B, S, N_HEADS, 2, HEAD_DIM), pos)
    return q, k, v.reshape(B, S, N_HEADS, V_DIM)


def _diff_weights(scores, mask, lam):
    p = jax.nn.softmax(jnp.where(mask, scores, -jnp.inf), axis=-1)
    return p[:, :, 0] - lam * p[:, :, 1]


def _prompt_diff_attn(q, k, v, lam):
    B, S = q.shape[:2]
    scale = HEAD_DIM ** -0.5
    kpos = jnp.arange(S)

    def block(i):
        qb = lax.dynamic_slice_in_dim(q, i * Q_BLOCK, Q_BLOCK, axis=1)
        s = jnp.einsum('bqhmd,bkhmd->bhmqk', qb, k, preferred_element_type=jnp.float32) * scale
        qpos = i * Q_BLOCK + jnp.arange(Q_BLOCK)
        a = _diff_weights(s, kpos[None, :] <= qpos[:, None], lam)
        return jnp.einsum('bhqk,bkhd->bqhd', a.astype(v.dtype), v)

    o = lax.map(block, jnp.arange(S // Q_BLOCK))
    return jnp.transpose(o, (1, 0, 2, 3, 4)).reshape(B, S, N_HEADS, V_DIM)


def _sample_diff_attn(q, k_new, v_new, cache_k, cache_v, page_table, layer_idx, lam):
    Bd, Sd = q.shape[:2]
    past = page_table.shape[1] * PAGE_SIZE
    scale = HEAD_DIM ** -0.5
    s_past = jnp.einsum('bqhmd,bkhmd->bhmqk', q,
                        cache_k[layer_idx, page_table].reshape(Bd, past, N_HEADS, 2, HEAD_DIM),
                        preferred_element_type=jnp.float32) * scale
    s_new = jnp.einsum('bqhmd,bkhmd->bhmqk', q, k_new, preferred_element_type=jnp.float32) * scale
    s = jnp.concatenate([s_past, s_new], axis=-1)
    mask = jnp.concatenate([jnp.ones((Sd, past), bool), jnp.tril(jnp.ones((Sd, Sd), bool))], axis=-1)
    a = _diff_weights(s, mask, lam).astype(v_new.dtype)
    o = jnp.einsum('bhqk,bkhd->bqhd', a[..., :past],
                   cache_v[layer_idx, page_table].reshape(Bd, past, N_HEADS, V_DIM))
    return o + jnp.einsum('bhqk,bkhd->bqhd', a[..., past:], v_new)


def _diff_out(o, subln_g, lam_init, w_o):
    B, S = o.shape[:2]
    o = _rmsnorm(o, subln_g) * (1 - lam_init)
    return o.reshape(B, S, N_HEADS * V_DIM) @ w_o


def _swiglu(h, wg, wu, wd):
    return (jax.nn.silu(h @ wg) * (h @ wu)) @ wd


def _moe(h, w_router, w_gate, w_up, w_down):
    logits = (h @ w_router).astype(jnp.float32)
    top_v, top_i = lax.top_k(logits, TOP_K)
    g = jax.nn.softmax(top_v, axis=-1)
    comb = jnp.sum(jax.nn.one_hot(top_i, N_EXPERTS, dtype=jnp.float32) * g[..., None], axis=-2)
    y = jnp.zeros_like(h)
    for e in range(N_EXPERTS):
        y = y + comb[..., e:e + 1].astype(h.dtype) * _swiglu(h, w_gate[e], w_up[e], w_down[e])
    return y


def setup_inputs(seed: int = 0) -> dict:
    key = jax.random.key(seed)
    ks = list(jax.random.split(key, 64))
    it = iter(ks)

    def nrm(shape, scale):
        return jax.random.normal(next(it), shape, jnp.float32) * scale

    n_pages = PAST_LEN // PAGE_SIZE
    used = DEC_BATCH * n_pages
    n_pool = used + max(1, used // 4)
    D = D_MODEL
    page_table = jax.random.permutation(next(it), n_pool)[:used].reshape(DEC_BATCH, n_pages).astype(jnp.int32)
    a0 = jax.random.uniform(next(it), (N_EVEN, RNN_WIDTH), jnp.float32, 0.9, 0.999)
    out_s = DEEPNORM_BETA
    return {
        "x_prompt": nrm((BATCH, SEQ, D), 1.0),
        "x_sample": nrm((DEC_BATCH, DEC_SEQ, D), 1.0),
        "state_rglru_conv": nrm((N_EVEN, DEC_BATCH, CONV_W - 1, RNN_WIDTH), 1.0),
        "state_rglru_h": nrm((N_EVEN, DEC_BATCH, RNN_WIDTH), 0.5),
        "cache_k": nrm((N_ODD, n_pool, PAGE_SIZE, N_HEADS, 2 * HEAD_DIM), 1.0),
        "cache_v": nrm((N_ODD, n_pool, PAGE_SIZE, N_HEADS, V_DIM), 1.0),
        "page_table": page_table,
        "c_prompt": nrm((BATCH, D), 1.0),
        "c_sample": nrm((DEC_BATCH, D), 1.0),
        "mix_mod_w": nrm((DEPTH, D, 3 * D), 0.5 * D ** -0.5),
        "mix_mod_b": nrm((DEPTH, 3 * D), 0.02),
        "mix_ln_g": 1.0 + nrm((DEPTH, D), 0.05),
        "mix_ln_b": nrm((DEPTH, D), 0.02),
        "ffn_mod_w": nrm((DEPTH, D, 3 * D), 0.5 * D ** -0.5),
        "ffn_mod_b": nrm((DEPTH, 3 * D), 0.02),
        "ffn_ln_g": 1.0 + nrm((DEPTH, D), 0.05),
        "ffn_ln_b": nrm((DEPTH, D), 0.02),
        "w_in0": nrm((N_EVEN, D, IN0_WIDTH), D ** -0.5),
        "conv_w": nrm((N_EVEN, CONV_W, RNN_WIDTH), CONV_W ** -0.5),
        "conv_b": nrm((N_EVEN, RNN_WIDTH), 0.02),
        "lru_wa": nrm((N_EVEN, RNN_HEADS, RNN_HEAD_DIM, RNN_HEAD_DIM), RNN_HEAD_DIM ** -0.5),
        "lru_ba": nrm((N_EVEN, RNN_WIDTH), 0.02),
        "lru_wx": nrm((N_EVEN, RNN_HEADS, RNN_HEAD_DIM, RNN_HEAD_DIM), RNN_HEAD_DIM ** -0.5),
        "lru_bx": nrm((N_EVEN, RNN_WIDTH), 0.02),
        "lru_lambda": jnp.log(a0) - jnp.log1p(-a0),
        "sgu_ln_g": 1.0 + nrm((N_EVEN, SGU_GROUPS, SGU_GROUP_DIM), 0.05),
        "sgu_ln_b": nrm((N_EVEN, SGU_GROUPS, SGU_GROUP_DIM), 0.02),
        "sgu_ws": nrm((N_EVEN, SGU_GROUPS, CHUNK, CHUNK), CHUNK ** -0.5),
        "sgu_bs": 1.0 + nrm((N_EVEN, SGU_GROUPS, CHUNK), 0.1),
        "w_out0": nrm((N_EVEN, MIX_WIDTH, D), out_s * MIX_WIDTH ** -0.5),
        "ffn_w_gate": nrm((N_EVEN, D, D_FF), D ** -0.5),
        "ffn_w_up": nrm((N_EVEN, D, D_FF), D ** -0.5),
        "ffn_w_down": nrm((N_EVEN, D_FF, D), out_s * D_FF ** -0.5),
        "w_qkv": nrm((N_ODD, D, 3 * D), D ** -0.5),
        "lambda_q1": nrm((N_ODD, HEAD_DIM), 0.1),
        "lambda_k1": nrm((N_ODD, HEAD_DIM), 0.1),
        "lambda_q2": nrm((N_ODD, HEAD_DIM), 0.1),
        "lambda_k2": nrm((N_ODD, HEAD_DIM), 0.1),
        "subln_g": 1.0 + nrm((N_ODD, V_DIM), 0.05),
        "w_o": nrm((N_ODD, N_HEADS * V_DIM, D), out_s * (N_HEADS * V_DIM) ** -0.5),
        "w_router": nrm((N_ODD, D, N_EXPERTS), D ** -0.5),
        "moe_w_gate": nrm((N_ODD, N_EXPERTS, D, D_FF_EXPERT), D ** -0.5),
        "moe_w_up": nrm((N_ODD, N_EXPERTS, D, D_FF_EXPERT), D ** -0.5),
        "moe_w_down": nrm((N_ODD, N_EXPERTS, D_FF_EXPERT, D), out_s * D_FF_EXPERT ** -0.5),
    }


def reference(x_prompt, x_sample, state_rglru_conv, state_rglru_h, cache_k, cache_v, page_table,
              c_prompt, c_sample,
              mix_mod_w, mix_mod_b, mix_ln_g, mix_ln_b, ffn_mod_w, ffn_mod_b, ffn_ln_g, ffn_ln_b,
              w_in0, conv_w, conv_b, lru_wa, lru_ba, lru_wx, lru_bx, lru_lambda,
              sgu_ln_g, sgu_ln_b, sgu_ws, sgu_bs, w_out0, ffn_w_gate, ffn_w_up, ffn_w_down,
              w_qkv, lambda_q1, lambda_k1, lambda_q2, lambda_k2, subln_g, w_o,
              w_router, moe_w_gate, moe_w_up, moe_w_down):
    B, S, _ = x_prompt.shape
    Bd, Sd, _ = x_sample.shape
    past = page_table.shape[1] * PAGE_SIZE
    pos_p = jnp.arange(S)
    pos_s = past + jnp.arange(Sd)
    xp, xs = x_prompt, x_sample
    conv_p, h_p, conv_s, h_s, sgu_v_s = [], [], [], [], []
    k_p, v_p, k_s, v_s = [], [], [], []
    for layer in range(DEPTH):
        e, o = layer // 2, layer // 2
        hp, gp = _modulate(xp, c_prompt, mix_mod_w[layer], mix_mod_b[layer])
        hs, gs = _modulate(xs, c_sample, mix_mod_w[layer], mix_mod_b[layer])
        if layer % 2 == 0:
            wts = (w_in0[e], conv_w[e], conv_b[e], lru_wa[e], lru_ba[e], lru_wx[e], lru_bx[e], lru_lambda[e],
                   sgu_ln_g[e], sgu_ln_b[e], sgu_ws[e], sgu_bs[e], w_out0[e])
            zb = jnp.zeros((B, CONV_W - 1, RNN_WIDTH), xp.dtype)
            zh = jnp.zeros((B, RNN_WIDTH), xp.dtype)
            op, nbp, nhp, _ = _rglru_sgu_mixer(hp, zb, zh, *wts)
            os_, nbs, nhs, vns = _rglru_sgu_mixer(hs, state_rglru_conv[e], state_rglru_h[e], *wts)
            conv_p.append(nbp)
            h_p.append(nhp)
            conv_s.append(nbs)
            h_s.append(nhs)
            sgu_v_s.append(vns)
        else:
            lam_init = 0.8 - 0.6 * math.exp(-0.3 * layer)
            lam = (jnp.exp(jnp.sum(lambda_q1[o].astype(jnp.float32) * lambda_k1[o].astype(jnp.float32)))
                   - jnp.exp(jnp.sum(lambda_q2[o].astype(jnp.float32) * lambda_k2[o].astype(jnp.float32)))
                   + lam_init)
            qp, kp, vp = _diff_qkv(hp, pos_p, w_qkv[o])
            op = _diff_out(_prompt_diff_attn(qp, kp, vp, lam), subln_g[o], lam_init, w_o[o])
            qs, kss, vss = _diff_qkv(hs, pos_s, w_qkv[o])
            att_s = _sample_diff_attn(qs, kss, vss, cache_k, cache_v, page_table, o, lam)
            os_ = _diff_out(att_s, subln_g[o], lam_init, w_o[o])
            k_p.append(kp.reshape(B, S, N_HEADS, 2 * HEAD_DIM))
            v_p.append(vp)
            k_s.append(kss.reshape(Bd, Sd, N_HEADS, 2 * HEAD_DIM))
            v_s.append(vss)
        xp = _post_norm(xp, op, gp, mix_ln_g[layer], mix_ln_b[layer])
        xs = _post_norm(xs, os_, gs, mix_ln_g[layer], mix_ln_b[layer])
        hp, gp = _modulate(xp, c_prompt, ffn_mod_w[layer], ffn_mod_b[layer])
        hs, gs = _modulate(xs, c_sample, ffn_mod_w[layer], ffn_mod_b[layer])
        if layer % 2 == 0:
            fp = _swiglu(hp, ffn_w_gate[e], ffn_w_up[e], ffn_w_down[e])
            fs = _swiglu(hs, ffn_w_gate[e], ffn_w_up[e], ffn_w_down[e])
        else:
            fp = _moe(hp, w_router[o], moe_w_gate[o], moe_w_up[o], moe_w_down[o])
            fs = _moe(hs, w_router[o], moe_w_gate[o], moe_w_up[o], moe_w_down[o])
        xp = _post_norm(xp, fp, gp, ffn_ln_g[layer], ffn_ln_b[layer])
        xs = _post_norm(xs, fs, gs, ffn_ln_g[layer], ffn_ln_b[layer])
    return (xp, xs, jnp.stack(conv_p), jnp.stack(h_p), jnp.stack(conv_s), jnp.stack(h_s), jnp.stack(sgu_v_s),
            jnp.stack(k_p), jnp.stack(v_p), jnp.stack(k_s), jnp.stack(v_s))
```

```python
import functools
import math

import jax
import jax.numpy as jnp
from jax import lax
from jax.experimental import pallas as pl
from jax.experimental.pallas import tpu as pltpu

F32 = jnp.float32
BF16 = jnp.bfloat16

D = 1024
SEQ = 2048
N_PROMPT = 8
N_SAMPLE = 32
N_COND = N_PROMPT + N_SAMPLE
PAGE = 128
RNN_W = 512
RNN_HALF = 256
SGU_W = 512
SGU_GROUPS = 4
GROUP_W = 128
CHUNK = 128
CONV_W = 4
LRU_C = 8.0
N_HEADS = 8
HEAD_W = 128
MAP_W = 64
ROT = 16
ROPE_THETA = 500000.0
D_FF = 2816
N_EXPERTS = 8
D_FF_E = 3584
LN_EPS = 1e-5
ALPHA = (2 * 2) ** 0.25
LAM_INIT = 0.8 - 0.6 * math.exp(-0.3 * 1)
ATT_SCALE = MAP_W ** -0.5
LANES = 128
NEG_INF = float("-inf")

VMEM_LIMIT = 56 * 1024 * 1024


def _params(sem):
    return pltpu.CompilerParams(dimension_semantics=sem, vmem_limit_bytes=VMEM_LIMIT)


def _dot(a, b):
    return jnp.dot(a, b, preferred_element_type=F32)


def _dot_nt(a, b):
    return lax.dot_general(a, b, (((1,), (1,)), ((), ())), preferred_element_type=F32)


def _layernorm(y, g, b):
    mu = jnp.mean(y, axis=-1, keepdims=True)
    yc = y - mu
    var = jnp.mean(yc * yc, axis=-1, keepdims=True)
    return yc * lax.rsqrt(var + LN_EPS) * g + b


def _silu(x):
    return x * jax.nn.sigmoid(x)


def _resident(shape):
    n = len(shape)
    return pl.BlockSpec(shape, lambda *_: (0,) * n, pipeline_mode=pl.Buffered(1))


def _row_spec(tm, width):
    return pl.BlockSpec((tm, width), lambda i: (i, 0))


def _mod_specs(mod, layer, parts, prompt, tm):
    if prompt:
        arr = mod.reshape(2, N_COND, 1, 3 * D)
        specs = [pl.BlockSpec((None, None, 1, D),
                              lambda i, *_, p=p: (layer, N_SAMPLE + (i * tm) // SEQ, 0, p)) for p in parts]
    else:
        arr = mod
        specs = [pl.BlockSpec((None, N_SAMPLE, D), lambda i, *_, p=p: (layer, 0, p)) for p in parts]
    return [arr] * len(parts), specs


def _cond_kernel(c_ref, w_ref, b_ref, o_ref):
    a = _silu(c_ref[...]).astype(BF16)
    o_ref[...] = _dot(a, w_ref[...].astype(BF16)) + b_ref[...]


def _cond_proj(c, w, b):
    tn = 512
    return pl.pallas_call(
        _cond_kernel,
        grid=(2, 3 * D // tn),
        in_specs=[pl.BlockSpec((N_COND, D), lambda l, j: (0, 0)),
                  pl.BlockSpec((None, D, tn), lambda l, j: (l, 0, j)),
                  pl.BlockSpec((None, 1, tn), lambda l, j: (l, 0, j))],
        out_specs=pl.BlockSpec((None, N_COND, tn), lambda l, j: (l, 0, j)),
        out_shape=jax.ShapeDtypeStruct((2, N_COND, 3 * D), F32),
        compiler_params=_params(("arbitrary", "arbitrary")),
        name="cond_proj",
    )(c, w, b.reshape(2, 1, 3 * D))


def _group_layernorm(v, g, b):
    outs = []
    for k in range(SGU_GROUPS):
        sl = slice(k * GROUP_W, (k + 1) * GROUP_W)
        outs.append(_layernorm(v[:, sl], g[:, sl], b[:, sl]))
    return outs


def _mix0_in_kernel(x_ref, sh_ref, sc_ref, w_ref, lng_ref, lnb_ref, ws_ref, bst_ref,
                    xr_ref, ggr_ref, sgu_ref, *, tm):
    h = (x_ref[...] * (1.0 + sc_ref[...]) + sh_ref[...]).astype(BF16)
    xr_ref[...] = _dot(h, w_ref[:, 0:RNN_W])
    ggr_ref[...] = jax.nn.gelu(_dot(h, w_ref[:, RNN_W:2 * RNN_W]))
    gu = jax.nn.gelu(_dot(h, w_ref[:, 2 * RNN_W:2 * RNN_W + SGU_W]))
    gv = jax.nn.gelu(_dot(h, w_ref[:, 2 * RNN_W + SGU_W:]))
    vn = _group_layernorm(gv, lng_ref[...], lnb_ref[...])
    row = lax.broadcasted_iota(jnp.int32, (CHUNK, CHUNK), 0)
    col = lax.broadcasted_iota(jnp.int32, (CHUNK, CHUNK), 1)
    for k in range(SGU_GROUPS):
        ws = jnp.where(col <= row, ws_ref[k], 0.0).astype(BF16)
        vk = vn[k].astype(BF16)
        bias = bst_ref[:, k:k + 1]
        for c in range(tm // CHUNK):
            rows = slice(c * CHUNK, (c + 1) * CHUNK)
            cols = slice(k * GROUP_W, (k + 1) * GROUP_W)
            s = _dot(ws, vk[rows]) + bias
            sgu_ref[rows, cols] = gu[rows, cols] * s


def _mix0_in(x, mod, w_in, ln_g, ln_b, ws, bs_t, tm=512):
    t = x.shape[0]
    mod_args, mod_specs = _mod_specs(mod, 0, (0, 1), True, tm)
    out = jax.ShapeDtypeStruct((t, RNN_W), F32)
    return pl.pallas_call(
        functools.partial(_mix0_in_kernel, tm=tm),
        grid=(t // tm,),
        in_specs=[_row_spec(tm, D), *mod_specs, _resident((D, 4 * RNN_W)),
                  _resident((1, SGU_W)), _resident((1, SGU_W)),
                  _resident((SGU_GROUPS, CHUNK, CHUNK)), _resident((CHUNK, SGU_GROUPS))],
        out_specs=[_row_spec(tm, RNN_W)] * 3,
        out_shape=[out] * 3,
        compiler_params=_params(("arbitrary",)),
        name="mix0_in",
    )(x, *mod_args, w_in, ln_g, ln_b, ws, bs_t)


def _lru_gates(xc, wa_ref, ba, wx_ref, bx, lam):
    xb = xc.astype(BF16)
    lo, hi = xb[:, :RNN_HALF], xb[:, RNN_HALF:]
    r = jax.nn.sigmoid(jnp.concatenate([_dot(lo, wa_ref[0]), _dot(hi, wa_ref[1])], axis=1) + ba)
    i = jax.nn.sigmoid(jnp.concatenate([_dot(lo, wx_ref[0]), _dot(hi, wx_ref[1])], axis=1) + bx)
    log_a = (-LRU_C) * r * jax.nn.softplus(-lam)
    a = jnp.exp(log_a)
    u = jnp.sqrt(-jnp.tanh(log_a) * (a * a + 1.0)) * (i * xc)
    return a, u


def _mix0_rnn_kernel(xr_ref, ggr_ref, sgu_ref, x_ref, gt_ref, cw_ref, cb_ref, wa_ref, ba_ref,
                     wx_ref, bx_ref, lam_ref, wo_ref, g_ref, b_ref,
                     y_ref, ht_ref, buf_ref, a_ref, u_ref, hs_ref, hc_ref, *, ts):
    t = pl.program_id(1)

    @pl.when(t == 0)
    def _():
        buf_ref[0:8, :] = jnp.zeros((8, RNN_W), F32)
        hc_ref[...] = jnp.zeros((1, RNN_W), F32)

    xr = xr_ref[...]
    buf_ref[8:8 + ts, :] = xr
    cw = cw_ref[...]
    xc = (buf_ref[5:5 + ts, :] * cw[0:1] + buf_ref[6:6 + ts, :] * cw[1:2]
          + buf_ref[7:7 + ts, :] * cw[2:3] + xr * cw[3:4]) + cb_ref[...]
    buf_ref[0:8, :] = xr[ts - 8:ts, :]

    a, u = _lru_gates(xc, wa_ref, ba_ref[...], wx_ref, bx_ref[...], lam_ref[...])
    a_ref[...] = a
    u_ref[...] = u

    row = lax.broadcasted_iota(jnp.int32, (8, RNN_W), 0)

    def tile(k, h_prev):
        off = pl.multiple_of(k * 8, 8)
        aa = a_ref[pl.ds(off, 8), :]
        uu = u_ref[pl.ds(off, 8), :]
        for d in (1, 2, 4):
            keep = row >= d
            a_sh = jnp.where(keep, pltpu.roll(aa, d, 0), 1.0)
            u_sh = jnp.where(keep, pltpu.roll(uu, d, 0), 0.0)
            uu = uu + aa * u_sh
            aa = aa * a_sh
        h = aa * h_prev + uu
        hs_ref[pl.ds(off, 8), :] = h
        return h[7:8, :]

    h_last = lax.fori_loop(0, ts // 8, tile, hc_ref[...], unroll=4)
    hc_ref[...] = h_last

    @pl.when(t == pl.num_programs(1) - 1)
    def _():
        ht_ref[...] = h_last

    rnn = (hs_ref[...] * ggr_ref[...]).astype(BF16)
    out = _dot(rnn, wo_ref[0:RNN_W, :]) + _dot(sgu_ref[...].astype(BF16), wo_ref[RNN_W:, :])
    y = ALPHA * x_ref[...] + (1.0 + gt_ref[...]) * out
    y_ref[...] = _layernorm(y, g_ref[...], b_ref[...])


def _mix0_rnn(xr, ggr, sgu, x, mod, conv_w, conv_b, wa, ba, wx, bx, lam, w_out, ln_g, ln_b, ts=512):
    nt = SEQ // ts
    arr = mod.reshape(2, N_COND, 1, 3 * D)
    tile_spec = lambda w: pl.BlockSpec((ts, w), lambda b, t: (b * nt + t, 0))
    res = lambda shape: pl.BlockSpec(shape, lambda b, t: (0,) * len(shape), pipeline_mode=pl.Buffered(1))
    return pl.pallas_call(
        functools.partial(_mix0_rnn_kernel, ts=ts),
        grid=(N_PROMPT, nt),
        in_specs=[tile_spec(RNN_W), tile_spec(RNN_W), tile_spec(SGU_W), tile_spec(D),
                  pl.BlockSpec((None, None, 1, D), lambda b, t: (0, N_SAMPLE + b, 0, 2)),
                  res((CONV_W, RNN_W)), res((1, RNN_W)),
                  res((2, RNN_HALF, RNN_HALF)), res((1, RNN_W)),
                  res((2, RNN_HALF, RNN_HALF)), res((1, RNN_W)), res((1, RNN_W)),
                  res((D, D)), res((1, D)), res((1, D))],
        out_specs=[tile_spec(D), pl.BlockSpec((None, 1, RNN_W), lambda b, t: (b, 0, 0))],
        out_shape=[jax.ShapeDtypeStruct((N_PROMPT * SEQ, D), F32),
                   jax.ShapeDtypeStruct((N_PROMPT, 1, RNN_W), F32)],
        scratch_shapes=[pltpu.VMEM((ts + 8, RNN_W), F32), pltpu.VMEM((ts, RNN_W), F32),
                        pltpu.VMEM((ts, RNN_W), F32), pltpu.VMEM((ts, RNN_W), F32),
                        pltpu.VMEM((1, RNN_W), F32)],
        compiler_params=_params(("arbitrary", "arbitrary")),
        name="mix0_rnn",
    )(xr, ggr, sgu, x, arr, conv_w, conv_b, wa, ba, wx, bx, lam, w_out, ln_g, ln_b)


def _mix0_sample_kernel(x_ref, sh_ref, sc_ref, gt_ref, w_ref, st_ref, h0_ref, cw_ref, cb_ref,
                        wa_ref, ba_ref, wx_ref, bx_ref, lam_ref, lng_ref, lnb_ref, ws0_ref, bs0_ref,
                        wo_ref, g_ref, b_ref, y_ref, xr_ref, ht_ref, vn_ref):
    x = x_ref[...]
    h = (x * (1.0 + sc_ref[...]) + sh_ref[...]).astype(BF16)
    xr = _dot(h, w_ref[:, 0:RNN_W])
    ggr = jax.nn.gelu(_dot(h, w_ref[:, RNN_W:2 * RNN_W]))
    gu = jax.nn.gelu(_dot(h, w_ref[:, 2 * RNN_W:2 * RNN_W + SGU_W]))
    gv = jax.nn.gelu(_dot(h, w_ref[:, 2 * RNN_W + SGU_W:]))
    xr_ref[...] = xr
    cw = cw_ref[...]
    xc = (st_ref[0] * cw[0:1] + st_ref[1] * cw[1:2] + st_ref[2] * cw[2:3] + xr * cw[3:4]) + cb_ref[...]
    a, u = _lru_gates(xc, wa_ref, ba_ref[...], wx_ref, bx_ref[...], lam_ref[...])
    h_new = a * h0_ref[...] + u
    ht_ref[...] = h_new
    vn = jnp.concatenate(_group_layernorm(gv, lng_ref[...], lnb_ref[...]), axis=1)
    vn_ref[...] = vn
    sgu = gu * (ws0_ref[...] * vn + bs0_ref[...])
    rnn = (h_new * ggr).astype(BF16)
    out = _dot(rnn, wo_ref[0:RNN_W, :]) + _dot(sgu.astype(BF16), wo_ref[RNN_W:, :])
    y = ALPHA * x + (1.0 + gt_ref[...]) * out
    y_ref[...] = _layernorm(y, g_ref[...], b_ref[...])


def _mix0_sample(x, mod, w_in, state, h0, conv_w, conv_b, wa, ba, wx, bx, lam, lng, lnb, ws0, bs0,
                 w_out, ln_g, ln_b):
    mod_args, mod_specs = _mod_specs(mod, 0, (0, 1, 2), False, N_SAMPLE)
    full = lambda a: pl.BlockSpec(a.shape, lambda i: (0,) * a.ndim)
    rest = (w_in, state, h0, conv_w, conv_b, wa, ba, wx, bx, lam, lng, lnb, ws0, bs0, w_out, ln_g, ln_b)
    vec = jax.ShapeDtypeStruct((N_SAMPLE, RNN_W), F32)
    return pl.pallas_call(
        _mix0_sample_kernel,
        grid=(1,),
        in_specs=[full(x), *mod_specs, *[full(a) for a in rest]],
        out_specs=[_row_spec(N_SAMPLE, D)] + [_row_spec(N_SAMPLE, RNN_W)] * 3,
        out_shape=[jax.ShapeDtypeStruct((N_SAMPLE, D), F32), vec, vec, vec],
        compiler_params=_params(("arbitrary",)),
        name="mix0_sample",
    )(x, *mod_args, *rest)


def _ffn_kernel(x_ref, sh_ref, sc_ref, gt_ref, wg_ref, wu_ref, wd_ref, g_ref, b_ref, y_ref, acc_ref,
                *, bounds):
    x = x_ref[...]
    h = (x * (1.0 + sc_ref[...]) + sh_ref[...]).astype(BF16)
    for n, (lo, hi) in enumerate(bounds):
        act = (_silu(_dot(h, wg_ref[:, lo:hi])) * _dot(h, wu_ref[:, lo:hi])).astype(BF16)
        part = _dot(act, wd_ref[lo:hi, :])
        if n == 0:
            acc_ref[...] = part
        else:
            acc_ref[...] += part
    y = ALPHA * x + (1.0 + gt_ref[...]) * acc_ref[...]
    y_ref[...] = _layernorm(y, g_ref[...], b_ref[...])


def _ffn(x, mod, prompt, wg, wu, wd, ln_g, ln_b, tm):
    t = x.shape[0]
    step = 512
    bounds = tuple((lo, min(lo + step, D_FF)) for lo in range(0, D_FF, step))
    mod_args, mod_specs = _mod_specs(mod, 0, (0, 1, 2), prompt, tm)
    return pl.pallas_call(
        functools.partial(_ffn_kernel, bounds=bounds),
        grid=(t // tm,),
        in_specs=[_row_spec(tm, D), *mod_specs, _resident((D, D_FF)), _resident((D, D_FF)),
                  _resident((D_FF, D)), _resident((1, D)), _resident((1, D))],
        out_specs=_row_spec(tm, D),
        out_shape=jax.ShapeDtypeStruct((t, D), F32),
        scratch_shapes=[pltpu.VMEM((tm, D), F32)],
        compiler_params=_params(("arbitrary",)),
        name="ffn_prompt" if prompt else "ffn_sample",
    )(x, *mod_args, wg, wu, wd, ln_g, ln_b)


def _rope(y, cos, sin_signed):
    lane = lax.broadcasted_iota(jnp.int32, y.shape, 1)
    first_half = (lane % MAP_W) < (ROT // 2)
    partner = jnp.where(first_half, pltpu.roll(y, HEAD_W - ROT // 2, 1), pltpu.roll(y, ROT // 2, 1))
    return y * cos + partner * sin_signed


def _qkv_kernel(x_ref, sh_ref, sc_ref, w_ref, cos_ref, sin_ref, q_ref, k_ref, v_ref):
    h = (x_ref[...] * (1.0 + sc_ref[...]) + sh_ref[...]).astype(BF16)
    cos, sin = cos_ref[...], sin_ref[...]
    width = 4 * HEAD_W
    for out_ref, base, rotate in ((q_ref, 0, True), (k_ref, D, True), (v_ref, 2 * D, False)):
        for j in range(D // width):
            y = _dot(h, w_ref[:, base + j * width:base + (j + 1) * width])
            if rotate:
                y = jnp.concatenate([_rope(y[:, n * HEAD_W:(n + 1) * HEAD_W], cos, sin)
                                     for n in range(width // HEAD_W)], axis=1)
            out_ref[:, j * width:(j + 1) * width] = y


def _qkv(x, mod, prompt, w_qkv, cos, sin, tm):
    t = x.shape[0]
    mod_args, mod_specs = _mod_specs(mod, 1, (0, 1), prompt, tm)
    if prompt:
        tab_spec = pl.BlockSpec((tm, HEAD_W), lambda i: (i % (SEQ // tm), 0))
    else:
        tab_spec = pl.BlockSpec((1, HEAD_W), lambda i: (0, 0))
    out = jax.ShapeDtypeStruct((t, D), F32)
    return pl.pallas_call(
        _qkv_kernel,
        grid=(t // tm,),
        in_specs=[_row_spec(tm, D), *mod_specs, _resident((D, 3 * D)), tab_spec, tab_spec],
        out_specs=[_row_spec(tm, D)] * 3,
        out_shape=[out] * 3,
        compiler_params=_params(("arbitrary",)),
        name="qkv_prompt" if prompt else "qkv_sample",
    )(x, *mod_args, w_qkv, cos, sin)


def _rope_tables(pos):
    half = ROT // 2
    inv = ROPE_THETA ** (-(jnp.arange(half, dtype=F32) * 2.0 / ROT))
    ang = pos.astype(F32)[:, None] * inv
    ones = jnp.ones((pos.shape[0], MAP_W - ROT), F32)
    cos = jnp.concatenate([jnp.cos(ang), jnp.cos(ang), ones], axis=1)
    sin = jnp.concatenate([-jnp.sin(ang), jnp.sin(ang), 0.0 * ones], axis=1)
    return jnp.tile(cos, (1, 2)), jnp.tile(sin, (1, 2))


def _diff_lambda(lp):
    s1 = jnp.sum(lp[0:1] * lp[1:2], axis=-1, keepdims=True)
    s2 = jnp.sum(lp[2:3] * lp[3:4], axis=-1, keepdims=True)
    return jnp.exp(s1) - jnp.exp(s2) + LAM_INIT


def _flash_kernel(lp_ref, q_ref, k_ref, v_ref, o_ref, qs_ref, m_ref, l_ref, acc_ref, *, tq):
    qi = pl.program_id(2)
    ki = pl.program_id(3)

    @pl.when(ki == 0)
    def _():
        q = q_ref[...] * ATT_SCALE
        lane = lax.broadcasted_iota(jnp.int32, q.shape, 1)
        qs_ref[0] = jnp.where(lane < MAP_W, q, 0.0).astype(BF16)
        qs_ref[1] = jnp.where(lane >= MAP_W, q, 0.0).astype(BF16)
        m_ref[...] = jnp.full(m_ref.shape, NEG_INF, F32)
        l_ref[...] = jnp.zeros(l_ref.shape, F32)
        acc_ref[...] = jnp.zeros(acc_ref.shape, F32)

    def step(diagonal):
        k = k_ref[...].astype(BF16)
        v = v_ref[...].astype(BF16)
        if diagonal:
            row = lax.broadcasted_iota(jnp.int32, (tq, tq), 0)
            col = lax.broadcasted_iota(jnp.int32, (tq, tq), 1)
            visible = col <= row
        for m in range(2):
            s = _dot_nt(qs_ref[m], k)
            if diagonal:
                s = jnp.where(visible, s, NEG_INF)
            m_prev = m_ref[m]
            m_new = jnp.maximum(m_prev, jnp.max(s, axis=-1, keepdims=True))
            alpha = jnp.exp(m_prev - m_new)
            p = jnp.exp(s - m_new)
            l_ref[m] = alpha * l_ref[m] + jnp.sum(p, axis=-1, keepdims=True)
            acc_ref[m] = alpha * acc_ref[m] + _dot(p.astype(BF16), v)
            m_ref[m] = m_new

    @pl.when(ki < qi)
    def _():
        step(False)

    @pl.when(ki == qi)
    def _():
        step(True)
        lam = _diff_lambda(lp_ref[...])
        o_ref[...] = acc_ref[0] / l_ref[0] - lam * (acc_ref[1] / l_ref[1])


def _flash(lp, q, k, v, tq=512):
    nq = SEQ // tq
    q3, k3, v3 = (a.reshape(N_PROMPT, SEQ, D) for a in (q, k, v))
    q_spec = pl.BlockSpec((None, tq, HEAD_W), lambda b, h, i, j: (b, i, h))
    kv_spec = pl.BlockSpec((None, tq, HEAD_W), lambda b, h, i, j: (b, jnp.minimum(i, j), h))
    return pl.pallas_call(
        functools.partial(_flash_kernel, tq=tq),
        grid=(N_PROMPT, N_HEADS, nq, nq),
        in_specs=[pl.BlockSpec((4, MAP_W), lambda b, h, i, j: (0, 0)), q_spec, kv_spec, kv_spec],
        out_specs=q_spec,
        out_shape=jax.ShapeDtypeStruct((N_PROMPT, SEQ, D), F32),
        scratch_shapes=[pltpu.VMEM((2, tq, HEAD_W), BF16), pltpu.VMEM((2, tq, 1), F32),
                        pltpu.VMEM((2, tq, 1), F32), pltpu.VMEM((2, tq, HEAD_W), F32)],
        compiler_params=_params(("arbitrary",) * 4),
        name="diff_attn_prompt",
    )(lp, q3, k3, v3).reshape(N_PROMPT * SEQ, D)


def _decode_kernel(pt_ref, lp_ref, q_ref, kn_ref, vn_ref, kc_ref, vc_ref, o_ref, qb_ref, m_ref, l_ref, acc_ref):
    j = pl.program_id(1)

    @pl.when(j == 0)
    def _():
        q = q_ref[...] * ATT_SCALE
        lane = lax.broadcasted_iota(jnp.int32, q.shape, 1)
        qb_ref[0:N_HEADS, :] = jnp.where(lane < MAP_W, q, 0.0)
        qb_ref[N_HEADS:, :] = jnp.where(lane >= MAP_W, q, 0.0)
        m_ref[...] = jnp.full(m_ref.shape, NEG_INF, F32)
        l_ref[...] = jnp.zeros(l_ref.shape, F32)
        acc_ref[...] = jnp.zeros(acc_ref.shape, F32)

    k2 = kc_ref[...].reshape(PAGE * N_HEADS, HEAD_W).astype(BF16)
    v2 = vc_ref[...].reshape(PAGE * N_HEADS, HEAD_W).astype(BF16)
    s = _dot_nt(qb_ref[...].astype(BF16), k2)
    r = lax.broadcasted_iota(jnp.int32, s.shape, 0)
    c = lax.broadcasted_iota(jnp.int32, s.shape, 1)
    s = jnp.where(c % N_HEADS == r % N_HEADS, s, NEG_INF)
    m_prev = m_ref[...]
    m_new = jnp.maximum(m_prev, jnp.max(s, axis=-1, keepdims=True))
    alpha = jnp.exp(m_prev - m_new)
    p = jnp.exp(s - m_new)
    l_ref[...] = alpha * l_ref[...] + jnp.sum(p, axis=-1, keepdims=True)
    acc_ref[...] = alpha * acc_ref[...] + _dot(p.astype(BF16), v2)
    m_ref[...] = m_new

    @pl.when(j == pl.num_programs(1) - 1)
    def _():
        k_new = jnp.concatenate([kn_ref[...], kn_ref[...]], axis=0)
        v_new = jnp.concatenate([vn_ref[...], vn_ref[...]], axis=0)
        s_new = jnp.sum(qb_ref[...] * k_new, axis=-1, keepdims=True)
        m_old = m_ref[...]
        m_fin = jnp.maximum(m_old, s_new)
        a_old = jnp.exp(m_old - m_fin)
        p_new = jnp.exp(s_new - m_fin)
        l_fin = a_old * l_ref[...] + p_new
        o_maps = (a_old * acc_ref[...] + p_new * v_new) / l_fin
        o_ref[...] = o_maps[0:N_HEADS] - _diff_lambda(lp_ref[...]) * o_maps[N_HEADS:]


def _decode(lp, q, k_new, v_new, cache_k, cache_v, page_table):
    n_pages = page_table.shape[1]
    kc = cache_k.reshape(-1, PAGE, N_HEADS, HEAD_W)
    vc = cache_v.reshape(-1, PAGE, N_HEADS, HEAD_W)
    heads = lambda a: a.reshape(N_SAMPLE, N_HEADS, HEAD_W)
    head_spec = pl.BlockSpec((None, N_HEADS, HEAD_W), lambda b, j, pt: (b, 0, 0))
    page_spec = pl.BlockSpec((None, PAGE, N_HEADS, HEAD_W), lambda b, j, pt: (pt[b, j], 0, 0, 0))
    n_maps = 2 * N_HEADS
    return pl.pallas_call(
        _decode_kernel,
        grid_spec=pltpu.PrefetchScalarGridSpec(
            num_scalar_prefetch=1,
            grid=(N_SAMPLE, n_pages),
            in_specs=[pl.BlockSpec((4, MAP_W), lambda b, j, pt: (0, 0)), head_spec, head_spec, head_spec,
                      page_spec, page_spec],
            out_specs=head_spec,
            scratch_shapes=[pltpu.VMEM((n_maps, HEAD_W), F32), pltpu.VMEM((n_maps, 1), F32),
                            pltpu.VMEM((n_maps, 1), F32), pltpu.VMEM((n_maps, HEAD_W), F32)]),
        out_shape=jax.ShapeDtypeStruct((N_SAMPLE, N_HEADS, HEAD_W), F32),
        compiler_params=_params(("arbitrary", "arbitrary")),
        name="diff_attn_decode",
    )(page_table, lp, heads(q), heads(k_new), heads(v_new), kc, vc).reshape(N_SAMPLE, D)


def _attn_out_kernel(o_ref, x_ref, gt_ref, sg_ref, wo_ref, g_ref, b_ref, y_ref):
    o = o_ref[...]
    heads = []
    for n in range(N_HEADS):
        oh = o[:, n * HEAD_W:(n + 1) * HEAD_W]
        ms = jnp.mean(oh * oh, axis=-1, keepdims=True)
        heads.append((oh * lax.rsqrt(ms + LN_EPS) * sg_ref[...]) * (1.0 - LAM_INIT))
    out = _dot(jnp.concatenate(heads, axis=1).astype(BF16), wo_ref[...])
    y = ALPHA * x_ref[...] + (1.0 + gt_ref[...]) * out
    y_ref[...] = _layernorm(y, g_ref[...], b_ref[...])


def _attn_out(o, x, mod, prompt, subln_g, w_o, ln_g, ln_b, tm):
    t = x.shape[0]
    mod_args, mod_specs = _mod_specs(mod, 1, (2,), prompt, tm)
    return pl.pallas_call(
        _attn_out_kernel,
        grid=(t // tm,),
        in_specs=[_row_spec(tm, D), _row_spec(tm, D), *mod_specs, _resident((1, HEAD_W)),
                  _resident((D, D)), _resident((1, D)), _resident((1, D))],
        out_specs=_row_spec(tm, D),
        out_shape=jax.ShapeDtypeStruct((t, D), F32),
        compiler_params=_params(("arbitrary",)),
        name="attn_out_prompt" if prompt else "attn_out_sample",
    )(o, x, *mod_args, subln_g, w_o, ln_g, ln_b)


def _route(logits):
    lane = lax.broadcasted_iota(jnp.int32, logits.shape, 1)
    z = jnp.where(lane < N_EXPERTS, logits, NEG_INF)
    m1 = jnp.max(z, axis=-1, keepdims=True)
    i1 = jnp.min(jnp.where(z == m1, lane, LANES), axis=-1, keepdims=True)
    z2 = jnp.where(lane == i1, NEG_INF, z)
    m2 = jnp.max(z2, axis=-1, keepdims=True)
    i2 = jnp.min(jnp.where(z2 == m2, lane, LANES), axis=-1, keepdims=True)
    e2 = jnp.exp(m2 - m1)
    g1 = 1.0 / (1.0 + e2)
    g2 = e2 / (1.0 + e2)
    return jnp.where(lane == i1, g1, 0.0) + jnp.where(lane == i2, g2, 0.0)


def _moe_kernel(x_ref, sh_ref, sc_ref, gt_ref, wr_ref, wg_ref, wu_ref, wd_ref, g_ref, b_ref, y_ref,
                h_ref, comb_ref, acc_ref):
    e = pl.program_id(1)
    f = pl.program_id(2)

    @pl.when((e == 0) & (f == 0))
    def _():
        h = (x_ref[...] * (1.0 + sc_ref[...]) + sh_ref[...]).astype(BF16)
        h_ref[...] = h
        comb_ref[...] = _route(_dot(h, wr_ref[...].astype(BF16)))
        acc_ref[...] = jnp.zeros(acc_ref.shape, F32)

    h = h_ref[...]
    act = (_silu(_dot(h, wg_ref[...].astype(BF16))) * _dot(h, wu_ref[...].astype(BF16))).astype(BF16)
    part = _dot(act, wd_ref[...].astype(BF16))
    lane = lax.broadcasted_iota(jnp.int32, comb_ref.shape, 1)
    weight = jnp.sum(jnp.where(lane == e, comb_ref[...], 0.0), axis=-1, keepdims=True)
    acc_ref[...] += weight * part

    @pl.when((e == pl.num_programs(1) - 1) & (f == pl.num_programs(2) - 1))
    def _():
        y = ALPHA * x_ref[...] + (1.0 + gt_ref[...]) * acc_ref[...]
        y_ref[...] = _layernorm(y, g_ref[...], b_ref[...])


def _moe(x, mod, prompt, w_router, wg, wu, wd, ln_g, ln_b, tm, tf=512):
    t = x.shape[0]
    arrs, specs = _mod_specs(mod, 1, (0, 1, 2), prompt, tm)
    row = pl.BlockSpec((tm, D), lambda i, e, f: (i, 0))
    small = lambda shape: pl.BlockSpec(shape, lambda i, e, f: (0,) * len(shape))
    return pl.pallas_call(
        _moe_kernel,
        grid=(t // tm, N_EXPERTS, D_FF_E // tf),
        in_specs=[row, *specs, small((D, LANES)),
                  pl.BlockSpec((None, D, tf), lambda i, e, f: (e, 0, f)),
                  pl.BlockSpec((None, D, tf), lambda i, e, f: (e, 0, f)),
                  pl.BlockSpec((None, tf, D), lambda i, e, f: (e, f, 0)),
                  small((1, D)), small((1, D))],
        out_specs=row,
        out_shape=jax.ShapeDtypeStruct((t, D), F32),
        scratch_shapes=[pltpu.VMEM((tm, D), BF16), pltpu.VMEM((tm, LANES), F32), pltpu.VMEM((tm, D), F32)],
        compiler_params=_params(("arbitrary",) * 3),
        name="moe_prompt" if prompt else "moe_sample",
    )(x, *arrs, w_router, wg, wu, wd, ln_g, ln_b)


def _block_diag(w):
    per = RNN_HALF // w.shape[-1]
    halves = [jax.scipy.linalg.block_diag(*[w[n * per + j] for j in range(per)]) for n in range(2)]
    return jnp.stack(halves).astype(BF16)


def kernel(x_prompt, x_sample, state_rglru_conv, state_rglru_h, cache_k, cache_v, page_table, c_prompt, c_sample, mix_mod_w, mix_mod_b, mix_ln_g, mix_ln_b, ffn_mod_w, ffn_mod_b, ffn_ln_g, ffn_ln_b, w_in0, conv_w, conv_b, lru_wa, lru_ba, lru_wx, lru_bx, lru_lambda, sgu_ln_g, sgu_ln_b, sgu_ws, sgu_bs, w_out0, ffn_w_gate, ffn_w_up, ffn_w_down, w_qkv, lambda_q1, lambda_k1, lambda_q2, lambda_k2, subln_g, w_o, w_router, moe_w_gate, moe_w_up, moe_w_down):
    past = page_table.shape[1] * PAGE
    xp = x_prompt.reshape(N_PROMPT * SEQ, D)
    xs = x_sample.reshape(N_SAMPLE, D)
    vec = lambda a: a.reshape(1, -1)

    cond = jnp.concatenate([c_sample, c_prompt], axis=0)
    mix_mod = _cond_proj(cond, mix_mod_w, mix_mod_b)
    ffn_mod = _cond_proj(cond, ffn_mod_w, ffn_mod_b)

    w_in = w_in0[0].astype(BF16)
    w_out = w_out0[0].astype(BF16)
    wa, wx = _block_diag(lru_wa[0]), _block_diag(lru_wx[0])
    lru = (conv_w[0], vec(conv_b[0]), wa, vec(lru_ba[0]), wx, vec(lru_bx[0]), vec(lru_lambda[0]))
    sgu_g, sgu_b = vec(sgu_ln_g[0]), vec(sgu_ln_b[0])
    ln0 = (vec(mix_ln_g[0]), vec(mix_ln_b[0]))

    xr_p, ggr_p, sgu_p = _mix0_in(xp, mix_mod, w_in, sgu_g, sgu_b, sgu_ws[0], sgu_bs[0].T)
    xp, ht_p = _mix0_rnn(xr_p, ggr_p, sgu_p, xp, mix_mod, *lru, w_out, *ln0)

    ws0 = vec(jnp.repeat(sgu_ws[0][:, 0, 0], GROUP_W))
    bs0 = vec(jnp.repeat(sgu_bs[0][:, 0], GROUP_W))
    state = jnp.swapaxes(state_rglru_conv[0], 0, 1)
    xs, xr_s, ht_s, vn_s = _mix0_sample(xs, mix_mod, w_in, state, state_rglru_h[0], *lru,
                                        sgu_g, sgu_b, ws0, bs0, w_out, *ln0)

    ffn_w = (ffn_w_gate[0].astype(BF16), ffn_w_up[0].astype(BF16), ffn_w_down[0].astype(BF16),
             vec(ffn_ln_g[0]), vec(ffn_ln_b[0]))
    xp = _ffn(xp, ffn_mod, True, *ffn_w, tm=512)
    xs = _ffn(xs, ffn_mod, False, *ffn_w, tm=N_SAMPLE)

    wqkv = w_qkv[0].astype(BF16)
    lp = jnp.stack([lambda_q1[0], lambda_k1[0], lambda_q2[0], lambda_k2[0]])
    cos_p, sin_p = _rope_tables(jnp.arange(SEQ))
    cos_s, sin_s = _rope_tables(jnp.full((1,), past))
    q_p, k_p, v_p = _qkv(xp, mix_mod, True, wqkv, cos_p, sin_p, tm=512)
    q_s, k_s, v_s = _qkv(xs, mix_mod, False, wqkv, cos_s, sin_s, tm=N_SAMPLE)
    o_p = _flash(lp, q_p, k_p, v_p)
    o_s = _decode(lp, q_s, k_s, v_s, cache_k, cache_v, page_table)
    att_w = (vec(subln_g[0]), w_o[0].astype(BF16), vec(mix_ln_g[1]), vec(mix_ln_b[1]))
    xp = _attn_out(o_p, xp, mix_mod, True, *att_w, tm=512)
    xs = _attn_out(o_s, xs, mix_mod, False, *att_w, tm=N_SAMPLE)

    w_r = jnp.pad(w_router[0], ((0, 0), (0, LANES - N_EXPERTS)))
    moe_w = (w_r, moe_w_gate[0], moe_w_up[0], moe_w_down[0], vec(ffn_ln_g[1]), vec(ffn_ln_b[1]))
    xp = _moe(xp, ffn_mod, True, *moe_w, tm=1024)
    xs = _moe(xs, ffn_mod, False, *moe_w, tm=N_SAMPLE)

    new_conv_s = jnp.concatenate([state_rglru_conv[0][:, 1:], xr_s[:, None]], axis=1)
    return (xp.reshape(N_PROMPT, SEQ, D), xs.reshape(N_SAMPLE, 1, D),
            xr_p.reshape(N_PROMPT, SEQ, RNN_W)[:, SEQ - (CONV_W - 1):][None],
            ht_p.reshape(1, N_PROMPT, RNN_W),
            new_conv_s[None], ht_s[None], vn_s.reshape(1, N_SAMPLE, 1, SGU_W),
            k_p.reshape(1, N_PROMPT, SEQ, N_HEADS, HEAD_W), v_p.reshape(1, N_PROMPT, SEQ, N_HEADS, HEAD_W),
            k_s.reshape(1, N_SAMPLE, 1, N_HEADS, HEAD_W), v_s.reshape(1, N_SAMPLE, 1, N_HEADS, HEAD_W))
```

```python
import functools
import math

import jax
import jax.numpy as jnp
from jax import lax
from jax.experimental import pallas as pl
from jax.experimental.pallas import tpu as pltpu

F32 = jnp.float32
BF16 = jnp.bfloat16

D = 1024
SEQ = 2048
N_PROMPT = 8
N_SAMPLE = 32
N_COND = N_PROMPT + N_SAMPLE
PAGE = 128
RNN_W = 512
RNN_HALF = 256
SGU_W = 512
SGU_GROUPS = 4
GROUP_W = 128
CHUNK = 128
CONV_W = 4
LRU_C = 8.0
N_HEADS = 8
HEAD_W = 128
MAP_W = 64
ROT = 16
ROPE_THETA = 500000.0
D_FF = 2816
N_EXPERTS = 8
D_FF_E = 3584
LN_EPS = 1e-5
ALPHA = (2 * 2) ** 0.25
LAM_INIT = 0.8 - 0.6 * math.exp(-0.3 * 1)
ATT_SCALE = MAP_W ** -0.5
LANES = 128
NEG_INF = float("-inf")

VMEM_LIMIT = 56 * 1024 * 1024


def _params(sem):
    return pltpu.CompilerParams(dimension_semantics=sem, vmem_limit_bytes=VMEM_LIMIT)


def _dot(a, b):
    return jnp.dot(a, b, preferred_element_type=F32)


def _dot_nt(a, b):
    return lax.dot_general(a, b, (((1,), (1,)), ((), ())), preferred_element_type=F32)


def _layernorm(y, g, b):
    mu = jnp.mean(y, axis=-1, keepdims=True)
    yc = y - mu
    var = jnp.mean(yc * yc, axis=-1, keepdims=True)
    return yc * lax.rsqrt(var + LN_EPS) * g + b


def _silu(x):
    return x * jax.nn.sigmoid(x)


def _resident(shape):
    n = len(shape)
    return pl.BlockSpec(shape, lambda *_: (0,) * n, pipeline_mode=pl.Buffered(1))


def _row_spec(tm, width):
    return pl.BlockSpec((tm, width), lambda i: (i, 0))


def _mod_specs(mod, layer, parts, prompt, tm):
    if prompt:
        arr = mod.reshape(2, N_COND, 1, 3 * D)
        specs = [pl.BlockSpec((None, None, 1, D),
                              lambda i, *_, p=p: (layer, N_SAMPLE + (i * tm) // SEQ, 0, p)) for p in parts]
    else:
        arr = mod
        specs = [pl.BlockSpec((None, N_SAMPLE, D), lambda i, *_, p=p: (layer, 0, p)) for p in parts]
    return [arr] * len(parts), specs


def _cond_kernel(c_ref, w_ref, b_ref, o_ref):
    a = _silu(c_ref[...]).astype(BF16)
    o_ref[...] = _dot(a, w_ref[...].astype(BF16)) + b_ref[...]


def _cond_proj(c, w, b):
    tn = 512
    return pl.pallas_call(
        _cond_kernel,
        grid=(2, 3 * D // tn),
        in_specs=[pl.BlockSpec((N_COND, D), lambda l, j: (0, 0)),
                  pl.BlockSpec((None, D, tn), lambda l, j: (l, 0, j)),
                  pl.BlockSpec((None, 1, tn), lambda l, j: (l, 0, j))],
        out_specs=pl.BlockSpec((None, N_COND, tn), lambda l, j: (l, 0, j)),
        out_shape=jax.ShapeDtypeStruct((2, N_COND, 3 * D), F32),
        compiler_params=_params(("arbitrary", "arbitrary")),
        name="cond_proj",
    )(c, w, b.reshape(2, 1, 3 * D))


def _group_layernorm(v, g, b):
    outs = []
    for k in range(SGU_GROUPS):
        sl = slice(k * GROUP_W, (k + 1) * GROUP_W)
        outs.append(_layernorm(v[:, sl], g[:, sl], b[:, sl]))
    return outs


def _mix0_in_kernel(x_ref, sh_ref, sc_ref, w_ref, lng_ref, lnb_ref, ws_ref, bst_ref,
                    xr_ref, ggr_ref, sgu_ref, *, tm):
    h = (x_ref[...] * (1.0 + sc_ref[...]) + sh_ref[...]).astype(BF16)
    xr_ref[...] = _dot(h, w_ref[:, 0:RNN_W])
    ggr_ref[...] = jax.nn.gelu(_dot(h, w_ref[:, RNN_W:2 * RNN_W]))
    gu = jax.nn.gelu(_dot(h, w_ref[:, 2 * RNN_W:2 * RNN_W + SGU_W]))
    gv = jax.nn.gelu(_dot(h, w_ref[:, 2 * RNN_W + SGU_W:]))
    vn = _group_layernorm(gv, lng_ref[...], lnb_ref[...])
    row = lax.broadcasted_iota(jnp.int32, (CHUNK, CHUNK), 0)
    col = lax.broadcasted_iota(jnp.int32, (CHUNK, CHUNK), 1)
    for k in range(SGU_GROUPS):
        ws = jnp.where(col <= row, ws_ref[k], 0.0).astype(BF16)
        vk = vn[k].astype(BF16)
        bias = bst_ref[:, k:k + 1]
        for c in range(tm // CHUNK):
            rows = slice(c * CHUNK, (c + 1) * CHUNK)
            cols = slice(k * GROUP_W, (k + 1) * GROUP_W)
            s = _dot(ws, vk[rows]) + bias
            sgu_ref[rows, cols] = gu[rows, cols] * s


def _mix0_in(x, mod, w_in, ln_g, ln_b, ws, bs_t, tm=512):
    t = x.shape[0]
    mod_args, mod_specs = _mod_specs(mod, 0, (0, 1), True, tm)
    out = jax.ShapeDtypeStruct((t, RNN_W), F32)
    return pl.pallas_call(
        functools.partial(_mix0_in_kernel, tm=tm),
        grid=(t // tm,),
        in_specs=[_row_spec(tm, D), *mod_specs, _resident((D, 4 * RNN_W)),
                  _resident((1, SGU_W)), _resident((1, SGU_W)),
                  _resident((SGU_GROUPS, CHUNK, CHUNK)), _resident((CHUNK, SGU_GROUPS))],
        out_specs=[_row_spec(tm, RNN_W)] * 3,
        out_shape=[out] * 3,
        compiler_params=_params(("arbitrary",)),
        name="mix0_in",
    )(x, *mod_args, w_in, ln_g, ln_b, ws, bs_t)


def _lru_gates(xc, wa_ref, ba, wx_ref, bx, lam):
    xb = xc.astype(BF16)
    lo, hi = xb[:, :RNN_HALF], xb[:, RNN_HALF:]
    r = jax.nn.sigmoid(jnp.concatenate([_dot(lo, wa_ref[0]), _dot(hi, wa_ref[1])], axis=1) + ba)
    i = jax.nn.sigmoid(jnp.concatenate([_dot(lo, wx_ref[0]), _dot(hi, wx_ref[1])], axis=1) + bx)
    log_a = (-LRU_C) * r * jax.nn.softplus(-lam)
    a = jnp.exp(log_a)
    u = jnp.sqrt(-jnp.tanh(log_a) * (a * a + 1.0)) * (i * xc)
    return a, u


def _mix0_rnn_kernel(xr_ref, ggr_ref, sgu_ref, x_ref, gt_ref, cw_ref, cb_ref, wa_ref, ba_ref,
                     wx_ref, bx_ref, lam_ref, wo_ref, g_ref, b_ref,
                     y_ref, ht_ref, buf_ref, a_ref, u_ref, hs_ref, hc_ref, *, ts):
    t = pl.program_id(1)

    @pl.when(t == 0)
    def _():
        buf_ref[0:8, :] = jnp.zeros((8, RNN_W), F32)
        hc_ref[...] = jnp.zeros((1, RNN_W), F32)

    xr = xr_ref[...]
    buf_ref[8:8 + ts, :] = xr
    cw = cw_ref[...]
    xc = (buf_ref[5:5 + ts, :] * cw[0:1] + buf_ref[6:6 + ts, :] * cw[1:2]
          + buf_ref[7:7 + ts, :] * cw[2:3] + xr * cw[3:4]) + cb_ref[...]
    buf_ref[0:8, :] = xr[ts - 8:ts, :]

    a, u = _lru_gates(xc, wa_ref, ba_ref[...], wx_ref, bx_ref[...], lam_ref[...])
    a_ref[...] = a
    u_ref[...] = u

    row = lax.broadcasted_iota(jnp.int32, (8, RNN_W), 0)

    def tile(k, h_prev):
        off = pl.multiple_of(k * 8, 8)
        aa = a_ref[pl.ds(off, 8), :]
        uu = u_ref[pl.ds(off, 8), :]
        for d in (1, 2, 4):
            keep = row >= d
            a_sh = jnp.where(keep, pltpu.roll(aa, d, 0), 1.0)
            u_sh = jnp.where(keep, pltpu.roll(uu, d, 0), 0.0)
            uu = uu + aa * u_sh
            aa = aa * a_sh
        h = aa * h_prev + uu
        hs_ref[pl.ds(off, 8), :] = h
        return h[7:8, :]

    h_last = lax.fori_loop(0, ts // 8, tile, hc_ref[...], unroll=4)
    hc_ref[...] = h_last

    @pl.when(t == pl.num_programs(1) - 1)
    def _():
        ht_ref[...] = h_last

    rnn = (hs_ref[...] * ggr_ref[...]).astype(BF16)
    out = _dot(rnn, wo_ref[0:RNN_W, :]) + _dot(sgu_ref[...].astype(BF16), wo_ref[RNN_W:, :])
    y = ALPHA * x_ref[...] + (1.0 + gt_ref[...]) * out
    y_ref[...] = _layernorm(y, g_ref[...], b_ref[...])


def _mix0_rnn(xr, ggr, sgu, x, mod, conv_w, conv_b, wa, ba, wx, bx, lam, w_out, ln_g, ln_b, ts=512):
    nt = SEQ // ts
    arr = mod.reshape(2, N_COND, 1, 3 * D)
    tile_spec = lambda w: pl.BlockSpec((ts, w), lambda b, t: (b * nt + t, 0))
    res = lambda shape: pl.BlockSpec(shape, lambda b, t: (0,) * len(shape), pipeline_mode=pl.Buffered(1))
    return pl.pallas_call(
        functools.partial(_mix0_rnn_kernel, ts=ts),
        grid=(N_PROMPT, nt),
        in_specs=[tile_spec(RNN_W), tile_spec(RNN_W), tile_spec(SGU_W), tile_spec(D),
                  pl.BlockSpec((None, None, 1, D), lambda b, t: (0, N_SAMPLE + b, 0, 2)),
                  res((CONV_W, RNN_W)), res((1, RNN_W)),
                  res((2, RNN_HALF, RNN_HALF)), res((1, RNN_W)),
                  res((2, RNN_HALF, RNN_HALF)), res((1, RNN_W)), res((1, RNN_W)),
                  res((D, D)), res((1, D)), res((1, D))],
        out_specs=[tile_spec(D), pl.BlockSpec((None, 1, RNN_W), lambda b, t: (b, 0, 0))],
        out_shape=[jax.ShapeDtypeStruct((N_PROMPT * SEQ, D), F32),
                   jax.ShapeDtypeStruct((N_PROMPT, 1, RNN_W), F32)],
        scratch_shapes=[pltpu.VMEM((ts + 8, RNN_W), F32), pltpu.VMEM((ts, RNN_W), F32),
                        pltpu.VMEM((ts, RNN_W), F32), pltpu.VMEM((ts, RNN_W), F32),
                        pltpu.VMEM((1, RNN_W), F32)],
        compiler_params=_params(("arbitrary", "arbitrary")),
        name="mix0_rnn",
    )(xr, ggr, sgu, x, arr, conv_w, conv_b, wa, ba, wx, bx, lam, w_out, ln_g, ln_b)


def _mix0_sample_kernel(x_ref, sh_ref, sc_ref, gt_ref, w_ref, st_ref, h0_ref, cw_ref, cb_ref,
                        wa_ref, ba_ref, wx_ref, bx_ref, lam_ref, lng_ref, lnb_ref, ws0_ref, bs0_ref,
                        wo_ref, g_ref, b_ref, y_ref, xr_ref, ht_ref, vn_ref):
    x = x_ref[...]
    h = (x * (1.0 + sc_ref[...]) + sh_ref[...]).astype(BF16)
    xr = _dot(h, w_ref[:, 0:RNN_W])
    ggr = jax.nn.gelu(_dot(h, w_ref[:, RNN_W:2 * RNN_W]))
    gu = jax.nn.gelu(_dot(h, w_ref[:, 2 * RNN_W:2 * RNN_W + SGU_W]))
    gv = jax.nn.gelu(_dot(h, w_ref[:, 2 * RNN_W + SGU_W:]))
    xr_ref[...] = xr
    cw = cw_ref[...]
    xc = (st_ref[0] * cw[0:1] + st_ref[1] * cw[1:2] + st_ref[2] * cw[2:3] + xr * cw[3:4]) + cb_ref[...]
    a, u = _lru_gates(xc, wa_ref, ba_ref[...], wx_ref, bx_ref[...], lam_ref[...])
    h_new = a * h0_ref[...] + u
    ht_ref[...] = h_new
    vn = jnp.concatenate(_group_layernorm(gv, lng_ref[...], lnb_ref[...]), axis=1)
    vn_ref[...] = vn
    sgu = gu * (ws0_ref[...] * vn + bs0_ref[...])
    rnn = (h_new * ggr).astype(BF16)
    out = _dot(rnn, wo_ref[0:RNN_W, :]) + _dot(sgu.astype(BF16), wo_ref[RNN_W:, :])
    y = ALPHA * x + (1.0 + gt_ref[...]) * out
    y_ref[...] = _layernorm(y, g_ref[...], b_ref[...])


def _mix0_sample(x, mod, w_in, state, h0, conv_w, conv_b, wa, ba, wx, bx, lam, lng, lnb, ws0, bs0,
                 w_out, ln_g, ln_b):
    mod_args, mod_specs = _mod_specs(mod, 0, (0, 1, 2), False, N_SAMPLE)
    full = lambda a: pl.BlockSpec(a.shape, lambda i: (0,) * a.ndim)
    rest = (w_in, state, h0, conv_w, conv_b, wa, ba, wx, bx, lam, lng, lnb, ws0, bs0, w_out, ln_g, ln_b)
    vec = jax.ShapeDtypeStruct((N_SAMPLE, RNN_W), F32)
    return pl.pallas_call(
        _mix0_sample_kernel,
        grid=(1,),
        in_specs=[full(x), *mod_specs, *[full(a) for a in rest]],
        out_specs=[_row_spec(N_SAMPLE, D)] + [_row_spec(N_SAMPLE, RNN_W)] * 3,
        out_shape=[jax.ShapeDtypeStruct((N_SAMPLE, D), F32), vec, vec, vec],
        compiler_params=_params(("arbitrary",)),
        name="mix0_sample",
    )(x, *mod_args, *rest)


def _ffn_kernel(x_ref, sh_ref, sc_ref, gt_ref, wg_ref, wu_ref, wd_ref, g_ref, b_ref, y_ref, acc_ref,
                *, bounds):
    x = x_ref[...]
    h = (x * (1.0 + sc_ref[...]) + sh_ref[...]).astype(BF16)
    for n, (lo, hi) in enumerate(bounds):
        act = (_silu(_dot(h, wg_ref[:, lo:hi])) * _dot(h, wu_ref[:, lo:hi])).astype(BF16)
        part = _dot(act, wd_ref[lo:hi, :])
        if n == 0:
            acc_ref[...] = part
        else:
            acc_ref[...] += part
    y = ALPHA * x + (1.0 + gt_ref[...]) * acc_ref[...]
    y_ref[...] = _layernorm(y, g_ref[...], b_ref[...])


def _ffn(x, mod, prompt, wg, wu, wd, ln_g, ln_b, tm):
    t = x.shape[0]
    step = 512
    bounds = tuple((lo, min(lo + step, D_FF)) for lo in range(0, D_FF, step))
    mod_args, mod_specs = _mod_specs(mod, 0, (0, 1, 2), prompt, tm)
    return pl.pallas_call(
        functools.partial(_ffn_kernel, bounds=bounds),
        grid=(t // tm,),
        in_specs=[_row_spec(tm, D), *mod_specs, _resident((D, D_FF)), _resident((D, D_FF)),
                  _resident((D_FF, D)), _resident((1, D)), _resident((1, D))],
        out_specs=_row_spec(tm, D),
        out_shape=jax.ShapeDtypeStruct((t, D), F32),
        scratch_shapes=[pltpu.VMEM((tm, D), F32)],
        compiler_params=_params(("arbitrary",)),
        name="ffn_prompt" if prompt else "ffn_sample",
    )(x, *mod_args, wg, wu, wd, ln_g, ln_b)


def _rope(y, cos, sin_signed):
    lane = lax.broadcasted_iota(jnp.int32, y.shape, 1)
    first_half = (lane % MAP_W) < (ROT // 2)
    partner = jnp.where(first_half, pltpu.roll(y, HEAD_W - ROT // 2, 1), pltpu.roll(y, ROT // 2, 1))
    return y * cos + partner * sin_signed


def _qkv_kernel(x_ref, sh_ref, sc_ref, w_ref, cos_ref, sin_ref, q_ref, k_ref, v_ref):
    h = (x_ref[...] * (1.0 + sc_ref[...]) + sh_ref[...]).astype(BF16)
    cos, sin = cos_ref[...], sin_ref[...]
    width = 4 * HEAD_W
    for out_ref, base, rotate in ((q_ref, 0, True), (k_ref, D, True), (v_ref, 2 * D, False)):
        for j in range(D // width):
            y = _dot(h, w_ref[:, base + j * width:base + (j + 1) * width])
            if rotate:
                y = jnp.concatenate([_rope(y[:, n * HEAD_W:(n + 1) * HEAD_W], cos, sin)
                                     for n in range(width // HEAD_W)], axis=1)
            out_ref[:, j * width:(j + 1) * width] = y


def _qkv(x, mod, prompt, w_qkv, cos, sin, tm):
    t = x.shape[0]
    mod_args, mod_specs = _mod_specs(mod, 1, (0, 1), prompt, tm)
    if prompt:
        tab_spec = pl.BlockSpec((tm, HEAD_W), lambda i: (i % (SEQ // tm), 0))
    else:
        tab_spec = pl.BlockSpec((1, HEAD_W), lambda i: (0, 0))
    out = jax.ShapeDtypeStruct((t, D), F32)
    return pl.pallas_call(
        _qkv_kernel,
        grid=(t // tm,),
        in_specs=[_row_spec(tm, D), *mod_specs, _resident((D, 3 * D)), tab_spec, tab_spec],
        out_specs=[_row_spec(tm, D)] * 3,
        out_shape=[out] * 3,
        compiler_params=_params(("arbitrary",)),
        name="qkv_prompt" if prompt else "qkv_sample",
    )(x, *mod_args, w_qkv, cos, sin)


def _rope_tables(pos):
    half = ROT // 2
    inv = ROPE_THETA ** (-(jnp.arange(half, dtype=F32) * 2.0 / ROT))
    ang = pos.astype(F32)[:, None] * inv
    ones = jnp.ones((pos.shape[0], MAP_W - ROT), F32)
    cos = jnp.concatenate([jnp.cos(ang), jnp.cos(ang), ones], axis=1)
    sin = jnp.concatenate([-jnp.sin(ang), jnp.sin(ang), 0.0 * ones], axis=1)
    return jnp.tile(cos, (1, 2)), jnp.tile(sin, (1, 2))


def _diff_lambda(lp):
    s1 = jnp.sum(lp[0:1] * lp[1:2], axis=-1, keepdims=True)
    s2 = jnp.sum(lp[2:3] * lp[3:4], axis=-1, keepdims=True)
    return jnp.exp(s1) - jnp.exp(s2) + LAM_INIT


def _flash_kernel(lp_ref, q_ref, k_ref, v_ref, o_ref, kb_ref, vb_ref, qs_ref, m_ref, acc_ref, *, tq):
    qi = pl.program_id(2)

    @pl.when(qi == 0)
    def _():
        kb_ref[...] = k_ref[...].astype(BF16)
        vb_ref[:, 0:HEAD_W] = v_ref[...].astype(BF16)
        vb_ref[:, HEAD_W:] = jnp.ones((SEQ, HEAD_W), BF16)

    q = q_ref[...] * ATT_SCALE
    lane = lax.broadcasted_iota(jnp.int32, q.shape, 1)
    qs_ref[0] = jnp.where(lane < MAP_W, q, 0.0).astype(BF16)
    qs_ref[1] = jnp.where(lane >= MAP_W, q, 0.0).astype(BF16)
    m_ref[...] = jnp.full(m_ref.shape, NEG_INF, F32)
    acc_ref[...] = jnp.zeros(acc_ref.shape, F32)

    def block(j, diagonal):
        off = pl.multiple_of(j * tq, tq)
        kj = kb_ref[pl.ds(off, tq), :]
        vj = vb_ref[pl.ds(off, tq), :]
        if diagonal:
            row = lax.broadcasted_iota(jnp.int32, (tq, tq), 0)
            col = lax.broadcasted_iota(jnp.int32, (tq, tq), 1)
            visible = col <= row
        for m in range(2):
            s = _dot_nt(qs_ref[m], kj)
            if diagonal:
                s = jnp.where(visible, s, NEG_INF)
            m_prev = m_ref[m]
            m_new = jnp.maximum(m_prev, jnp.max(s, axis=-1, keepdims=True))
            p = jnp.exp(s - m_new).astype(BF16)
            acc_ref[m] = jnp.exp(m_prev - m_new) * acc_ref[m] + _dot(p, vj)
            m_ref[m] = m_new

    def below_diagonal(j, carry):
        block(j, False)
        return carry

    lax.fori_loop(0, qi, below_diagonal, 0)
    block(qi, True)
    o1 = acc_ref[0][:, 0:HEAD_W] / acc_ref[0][:, HEAD_W:]
    o2 = acc_ref[1][:, 0:HEAD_W] / acc_ref[1][:, HEAD_W:]
    o_ref[...] = o1 - _diff_lambda(lp_ref[...]) * o2


def _flash(lp, q, k, v, tq=512):
    nq = SEQ // tq
    q3, k3, v3 = (a.reshape(N_PROMPT, SEQ, D) for a in (q, k, v))
    q_spec = pl.BlockSpec((None, tq, HEAD_W), lambda b, h, i: (b, i, h))
    kv_spec = pl.BlockSpec((None, SEQ, HEAD_W), lambda b, h, i: (b, 0, h))
    return pl.pallas_call(
        functools.partial(_flash_kernel, tq=tq),
        grid=(N_PROMPT, N_HEADS, nq),
        in_specs=[pl.BlockSpec((4, MAP_W), lambda b, h, i: (0, 0)), q_spec, kv_spec, kv_spec],
        out_specs=q_spec,
        out_shape=jax.ShapeDtypeStruct((N_PROMPT, SEQ, D), F32),
        scratch_shapes=[pltpu.VMEM((SEQ, HEAD_W), BF16), pltpu.VMEM((SEQ, 2 * HEAD_W), BF16),
                        pltpu.VMEM((2, tq, HEAD_W), BF16), pltpu.VMEM((2, tq, 1), F32),
                        pltpu.VMEM((2, tq, 2 * HEAD_W), F32)],
        compiler_params=_params(("arbitrary",) * 3),
        name="diff_attn_prompt",
    )(lp, q3, k3, v3).reshape(N_PROMPT * SEQ, D)


def _decode_kernel(pt_ref, lp_ref, q_ref, kn_ref, vn_ref, *refs, n_pg):
    kc_refs, vc_refs = refs[:n_pg], refs[n_pg:2 * n_pg]
    o_ref, qb_ref, m_ref, l_ref, acc_ref = refs[2 * n_pg:]
    j = pl.program_id(1)

    @pl.when(j == 0)
    def _():
        q = q_ref[...] * ATT_SCALE
        lane = lax.broadcasted_iota(jnp.int32, q.shape, 1)
        qb_ref[0:N_HEADS, :] = jnp.where(lane < MAP_W, q, 0.0)
        qb_ref[N_HEADS:, :] = jnp.where(lane >= MAP_W, q, 0.0)
        m_ref[...] = jnp.full(m_ref.shape, NEG_INF, F32)
        l_ref[...] = jnp.zeros(l_ref.shape, F32)
        acc_ref[...] = jnp.zeros(acc_ref.shape, F32)

    rows = PAGE * N_HEADS
    qb = qb_ref[...].astype(BF16)
    r = lax.broadcasted_iota(jnp.int32, (2 * N_HEADS, rows), 0)
    c = lax.broadcasted_iota(jnp.int32, (2 * N_HEADS, rows), 1)
    own_head = c % N_HEADS == r % N_HEADS
    scores = []
    for kc_ref in kc_refs:
        s = _dot_nt(qb, kc_ref[...].reshape(rows, HEAD_W).astype(BF16))
        scores.append(jnp.where(own_head, s, NEG_INF))
    m_prev = m_ref[...]
    m_new = m_prev
    for s in scores:
        m_new = jnp.maximum(m_new, jnp.max(s, axis=-1, keepdims=True))
    l_new = jnp.exp(m_prev - m_new) * l_ref[...]
    acc = jnp.exp(m_prev - m_new) * acc_ref[...]
    for s, vc_ref in zip(scores, vc_refs):
        p = jnp.exp(s - m_new)
        l_new = l_new + jnp.sum(p, axis=-1, keepdims=True)
        acc = acc + _dot(p.astype(BF16), vc_ref[...].reshape(rows, HEAD_W).astype(BF16))
    l_ref[...] = l_new
    acc_ref[...] = acc
    m_ref[...] = m_new

    @pl.when(j == pl.num_programs(1) - 1)
    def _():
        k_new = jnp.concatenate([kn_ref[...], kn_ref[...]], axis=0)
        v_new = jnp.concatenate([vn_ref[...], vn_ref[...]], axis=0)
        s_new = jnp.sum(qb_ref[...] * k_new, axis=-1, keepdims=True)
        m_old = m_ref[...]
        m_fin = jnp.maximum(m_old, s_new)
        a_old = jnp.exp(m_old - m_fin)
        p_new = jnp.exp(s_new - m_fin)
        l_fin = a_old * l_ref[...] + p_new
        o_maps = (a_old * acc_ref[...] + p_new * v_new) / l_fin
        o_ref[...] = o_maps[0:N_HEADS] - _diff_lambda(lp_ref[...]) * o_maps[N_HEADS:]


def _decode(lp, q, k_new, v_new, cache_k, cache_v, page_table, n_pg=8):
    n_pages = page_table.shape[1]
    assert n_pages % n_pg == 0
    kc = cache_k.reshape(-1, PAGE, N_HEADS, HEAD_W)
    vc = cache_v.reshape(-1, PAGE, N_HEADS, HEAD_W)
    heads = lambda a: a.reshape(N_SAMPLE, N_HEADS, HEAD_W)
    head_spec = pl.BlockSpec((None, N_HEADS, HEAD_W), lambda b, j, pt: (b, 0, 0))
    page_specs = [pl.BlockSpec((None, PAGE, N_HEADS, HEAD_W),
                               lambda b, j, pt, g=g: (pt[b, j * n_pg + g], 0, 0, 0)) for g in range(n_pg)]
    n_maps = 2 * N_HEADS
    return pl.pallas_call(
        functools.partial(_decode_kernel, n_pg=n_pg),
        grid_spec=pltpu.PrefetchScalarGridSpec(
            num_scalar_prefetch=1,
            grid=(N_SAMPLE, n_pages // n_pg),
            in_specs=[pl.BlockSpec((4, MAP_W), lambda b, j, pt: (0, 0)), head_spec, head_spec, head_spec,
                      *page_specs, *page_specs],
            out_specs=head_spec,
            scratch_shapes=[pltpu.VMEM((n_maps, HEAD_W), F32), pltpu.VMEM((n_maps, 1), F32),
                            pltpu.VMEM((n_maps, 1), F32), pltpu.VMEM((n_maps, HEAD_W), F32)]),
        out_shape=jax.ShapeDtypeStruct((N_SAMPLE, N_HEADS, HEAD_W), F32),
        compiler_params=_params(("arbitrary", "arbitrary")),
        name="diff_attn_decode",
    )(page_table, lp, heads(q), heads(k_new), heads(v_new), *[kc] * n_pg, *[vc] * n_pg).reshape(N_SAMPLE, D)


def _attn_out_kernel(o_ref, x_ref, gt_ref, sg_ref, wo_ref, g_ref, b_ref, y_ref):
    o = o_ref[...]
    heads = []
    for n in range(N_HEADS):
        oh = o[:, n * HEAD_W:(n + 1) * HEAD_W]
        ms = jnp.mean(oh * oh, axis=-1, keepdims=True)
        heads.append((oh * lax.rsqrt(ms + LN_EPS) * sg_ref[...]) * (1.0 - LAM_INIT))
    out = _dot(jnp.concatenate(heads, axis=1).astype(BF16), wo_ref[...])
    y = ALPHA * x_ref[...] + (1.0 + gt_ref[...]) * out
    y_ref[...] = _layernorm(y, g_ref[...], b_ref[...])


def _attn_out(o, x, mod, prompt, subln_g, w_o, ln_g, ln_b, tm):
    t = x.shape[0]
    mod_args, mod_specs = _mod_specs(mod, 1, (2,), prompt, tm)
    return pl.pallas_call(
        _attn_out_kernel,
        grid=(t // tm,),
        in_specs=[_row_spec(tm, D), _row_spec(tm, D), *mod_specs, _resident((1, HEAD_W)),
                  _resident((D, D)), _resident((1, D)), _resident((1, D))],
        out_specs=_row_spec(tm, D),
        out_shape=jax.ShapeDtypeStruct((t, D), F32),
        compiler_params=_params(("arbitrary",)),
        name="attn_out_prompt" if prompt else "attn_out_sample",
    )(o, x, *mod_args, subln_g, w_o, ln_g, ln_b)


ROUTE_IDX = N_EXPERTS
ROUTE_GATE = N_EXPERTS + 2
MOE_TM = 1024
MOE_HALF = MOE_TM // 2
MOE_TF = 512
ROW_TILE = 256


def _route(logits):
    lane = lax.broadcasted_iota(jnp.int32, logits.shape, 1)
    z = jnp.where(lane < N_EXPERTS, logits, NEG_INF)
    m1 = jnp.max(z, axis=-1, keepdims=True)
    i1 = jnp.min(jnp.where(z == m1, lane, LANES), axis=-1, keepdims=True)
    z2 = jnp.where(lane == i1, NEG_INF, z)
    m2 = jnp.max(z2, axis=-1, keepdims=True)
    i2 = jnp.min(jnp.where(z2 == m2, lane, LANES), axis=-1, keepdims=True)
    e2 = jnp.exp(m2 - m1)
    g1 = 1.0 / (1.0 + e2)
    g2 = e2 / (1.0 + e2)
    out = jnp.where(lane == ROUTE_IDX, i1.astype(F32), 0.0) + jnp.where(lane == ROUTE_IDX + 1, i2.astype(F32), 0.0)
    return out + jnp.where(lane == ROUTE_GATE, g1, 0.0) + jnp.where(lane == ROUTE_GATE + 1, g2, 0.0)


def _router_kernel(x_ref, sh_ref, sc_ref, wr_ref, r_ref):
    h = (x_ref[...] * (1.0 + sc_ref[...]) + sh_ref[...]).astype(BF16)
    r_ref[...] = _route(_dot(h, wr_ref[...].astype(BF16)))


def _router(x, mod, prompt, w_router, tm):
    t = x.shape[0]
    arrs, specs = _mod_specs(mod, 1, (0, 1), prompt, tm)
    return pl.pallas_call(
        _router_kernel,
        grid=(t // tm,),
        in_specs=[_row_spec(tm, D), *specs, _resident((D, LANES))],
        out_specs=_row_spec(tm, LANES),
        out_shape=jax.ShapeDtypeStruct((t, LANES), F32),
        compiler_params=_params(("arbitrary",)),
        name="router_prompt" if prompt else "router_sample",
    )(x, *arrs, w_router)


def _moe_plan(expert_ids, tm, n_tiles):
    onehot = (expert_ids[:, None] == jnp.arange(N_EXPERTS, dtype=jnp.int32)[None, :]).astype(jnp.int32)
    rank = jnp.cumsum(onehot, axis=0) - onehot
    counts = jnp.sum(onehot, axis=0)
    tiles_per = (counts + tm - 1) // tm
    tile_end = jnp.cumsum(tiles_per)
    tile_start = tile_end - tiles_per
    pos = jnp.sum(onehot * (tile_start * tm + rank), axis=1)
    n_used = tile_end[-1]
    tile = jnp.minimum(jnp.arange(n_tiles, dtype=jnp.int32), n_used - 1)
    expert = jnp.sum((tile[:, None] >= tile_end[None, :]).astype(jnp.int32), axis=1)
    rows = jnp.clip(counts[expert] - (tile - tile_start[expert]) * tm, 0, tm)
    return pos.astype(jnp.int32), tile, expert.astype(jnp.int32), rows.astype(jnp.int32), n_used.reshape(1)


def _row_copy_wait(src_ref, dst_ref, sem, n):
    def one(_, carry):
        pltpu.make_async_copy(src_ref.at[pl.ds(0, 1), :], dst_ref.at[pl.ds(0, 1), :], sem).wait()
        return carry
    lax.fori_loop(0, n, one, 0)


def _dispatch_kernel(pos_ref, x_ref, sh_ref, sc_ref, xg_in_ref, xg_ref, h_ref, sem, *, tr, n_tok):
    del xg_in_ref
    i = pl.program_id(0)
    n = pl.num_programs(0)
    slot = i % 2
    n_copies = 2 * tr

    @pl.when(i >= 2)
    def _():
        _row_copy_wait(h_ref.at[slot], xg_ref, sem.at[slot], n_copies)

    h_ref[slot] = x_ref[...] * (1.0 + sc_ref[...]) + sh_ref[...]

    def send(r, carry):
        for k in range(2):
            dst = pos_ref[k * n_tok + i * tr + r]
            pltpu.make_async_copy(h_ref.at[slot, pl.ds(r, 1), :], xg_ref.at[pl.ds(dst, 1), :], sem.at[slot]).start()
        return carry

    lax.fori_loop(0, tr, send, 0)

    @pl.when(i == n - 1)
    def _():
        @pl.when(i >= 1)
        def _():
            _row_copy_wait(h_ref.at[1 - slot], xg_ref, sem.at[1 - slot], n_copies)
        _row_copy_wait(h_ref.at[slot], xg_ref, sem.at[slot], n_copies)


def _dispatch(x, mod, prompt, pos, xg, tr):
    t = x.shape[0]
    arrs, specs = _mod_specs(mod, 1, (0, 1), prompt, tr)
    any_spec = pl.BlockSpec(memory_space=pl.ANY)
    return pl.pallas_call(
        functools.partial(_dispatch_kernel, tr=tr, n_tok=t),
        grid_spec=pltpu.PrefetchScalarGridSpec(
            num_scalar_prefetch=1,
            grid=(t // tr,),
            in_specs=[pl.BlockSpec((tr, D), lambda i, pos: (i, 0)), *specs, any_spec],
            out_specs=any_spec,
            scratch_shapes=[pltpu.VMEM((2, tr, D), F32), pltpu.SemaphoreType.DMA((2,))]),
        out_shape=jax.ShapeDtypeStruct(xg.shape, F32),
        input_output_aliases={4: 0},
        compiler_params=_params(("arbitrary",)),
        name="moe_dispatch_prompt" if prompt else "moe_dispatch_sample",
    )(pos, x, *arrs, xg)


def _expert_kernel(tile_ref, exp_ref, rows_ref, used_ref, xg_ref, wg_ref, wu_ref, wd_ref, yg_ref, hb_ref):
    i = pl.program_id(0)
    f = pl.program_id(1)
    live = i < used_ref[0]

    @pl.when(live & (f == 0))
    def _():
        hb_ref[...] = xg_ref[...].astype(BF16)

    for half in range(MOE_TM // MOE_HALF):
        rows = slice(half * MOE_HALF, (half + 1) * MOE_HALF)
        occupied = rows_ref[i] > half * MOE_HALF

        @pl.when(live & occupied)
        def _():
            h = hb_ref[rows, :]
            act = (_silu(_dot(h, wg_ref[...].astype(BF16))) * _dot(h, wu_ref[...].astype(BF16))).astype(BF16)
            part = _dot(act, wd_ref[...].astype(BF16))

            @pl.when(f == 0)
            def _():
                yg_ref[rows, :] = part

            @pl.when(f > 0)
            def _():
                yg_ref[rows, :] += part

        @pl.when(jnp.logical_not(live & occupied) & (f == 0))
        def _():
            yg_ref[rows, :] = jnp.zeros((MOE_HALF, D), F32)


def _experts(xg, tile, expert, rows, n_used, wg, wu, wd):
    n_tiles = xg.shape[0] // MOE_TM
    nf = D_FF_E // MOE_TF

    def f_block(i, f, used):
        return jnp.where(i < used[0], f, nf - 1)

    return pl.pallas_call(
        _expert_kernel,
        grid_spec=pltpu.PrefetchScalarGridSpec(
            num_scalar_prefetch=4,
            grid=(n_tiles, nf),
            in_specs=[pl.BlockSpec((MOE_TM, D), lambda i, f, tile, exp, rows, used: (tile[i], 0)),
                      pl.BlockSpec((None, D, MOE_TF),
                                   lambda i, f, tile, exp, rows, used: (exp[i], 0, f_block(i, f, used))),
                      pl.BlockSpec((None, D, MOE_TF),
                                   lambda i, f, tile, exp, rows, used: (exp[i], 0, f_block(i, f, used))),
                      pl.BlockSpec((None, MOE_TF, D),
                                   lambda i, f, tile, exp, rows, used: (exp[i], f_block(i, f, used), 0))],
            out_specs=pl.BlockSpec((MOE_TM, D), lambda i, f, tile, exp, rows, used: (i, 0)),
            scratch_shapes=[pltpu.VMEM((MOE_TM, D), BF16)]),
        out_shape=jax.ShapeDtypeStruct(xg.shape, F32),
        compiler_params=_params(("arbitrary", "arbitrary")),
        name="moe_experts",
    )(tile, expert, rows, n_used, xg, wg, wu, wd)


def _combine_kernel(pos_ref, yg_ref, r_ref, x_ref, gt_ref, g_ref, b_ref, y_ref, buf_ref, sem, *, tr, n_tok):
    i = pl.program_id(0)
    n = pl.num_programs(0)
    slot = i % 2

    def fetch(step, into):
        def one(r, carry):
            for k in range(2):
                src = pos_ref[k * n_tok + step * tr + r]
                pltpu.make_async_copy(yg_ref.at[pl.ds(src, 1), :], buf_ref.at[into, k, pl.ds(r, 1), :],
                                      sem.at[into]).start()
            return carry
        lax.fori_loop(0, tr, one, 0)

    @pl.when(i == 0)
    def _():
        fetch(0, 0)

    @pl.when(i + 1 < n)
    def _():
        fetch(i + 1, 1 - slot)

    _row_copy_wait(yg_ref, buf_ref.at[slot, 0], sem.at[slot], 2 * tr)
    route = r_ref[...]
    moe = route[:, ROUTE_GATE:ROUTE_GATE + 1] * buf_ref[slot, 0] + route[:, ROUTE_GATE + 1:ROUTE_GATE + 2] * buf_ref[slot, 1]
    y = ALPHA * x_ref[...] + (1.0 + gt_ref[...]) * moe
    y_ref[...] = _layernorm(y, g_ref[...], b_ref[...])


def _combine(yg, pos, route, x, mod, prompt, ln_g, ln_b, tr):
    t = x.shape[0]
    arrs, specs = _mod_specs(mod, 1, (2,), prompt, tr)
    return pl.pallas_call(
        functools.partial(_combine_kernel, tr=tr, n_tok=t),
        grid_spec=pltpu.PrefetchScalarGridSpec(
            num_scalar_prefetch=1,
            grid=(t // tr,),
            in_specs=[pl.BlockSpec(memory_space=pl.ANY),
                      pl.BlockSpec((tr, LANES), lambda i, pos: (i, 0)),
                      pl.BlockSpec((tr, D), lambda i, pos: (i, 0)), *specs,
                      pl.BlockSpec((1, D), lambda i, pos: (0, 0)), pl.BlockSpec((1, D), lambda i, pos: (0, 0))],
            out_specs=pl.BlockSpec((tr, D), lambda i, pos: (i, 0)),
            scratch_shapes=[pltpu.VMEM((2, 2, tr, D), F32), pltpu.SemaphoreType.DMA((2,))]),
        out_shape=jax.ShapeDtypeStruct((t, D), F32),
        compiler_params=_params(("arbitrary",)),
        name="moe_combine_prompt" if prompt else "moe_combine_sample",
    )(pos, yg, route, x, *arrs, ln_g, ln_b)


def _moe(xp, xs, mod, w_router, wg, wu, wd, ln_g, ln_b):
    tp, ts = xp.shape[0], xs.shape[0]
    w_r = jnp.pad(w_router, ((0, 0), (0, LANES - N_EXPERTS)))
    route_p = _router(xp, mod, True, w_r, tm=512)
    route_s = _router(xs, mod, False, w_r, tm=ts)
    ids = lambda r: r[:, ROUTE_IDX:ROUTE_IDX + 2].astype(jnp.int32).T.reshape(-1)
    n_pairs = 2 * (tp + ts)
    n_tiles = (n_pairs + N_EXPERTS * (MOE_TM - 1)) // MOE_TM
    pos, tile, expert, rows, n_used = _moe_plan(jnp.concatenate([ids(route_p), ids(route_s)]), MOE_TM, n_tiles)
    pos_p, pos_s = pos[:2 * tp], pos[2 * tp:]
    xg = jnp.zeros((n_tiles * MOE_TM, D), F32)
    xg = _dispatch(xp, mod, True, pos_p, xg, ROW_TILE)
    xg = _dispatch(xs, mod, False, pos_s, xg, ts)
    yg = _experts(xg, tile, expert, rows, n_used, wg, wu, wd)
    yp = _combine(yg, pos_p, route_p, xp, mod, True, ln_g, ln_b, ROW_TILE)
    ys = _combine(yg, pos_s, route_s, xs, mod, False, ln_g, ln_b, ts)
    return yp, ys


def _block_diag(w):
    per = RNN_HALF // w.shape[-1]
    halves = [jax.scipy.linalg.block_diag(*[w[n * per + j] for j in range(per)]) for n in range(2)]
    return jnp.stack(halves).astype(BF16)


def kernel(x_prompt, x_sample, state_rglru_conv, state_rglru_h, cache_k, cache_v, page_table, c_prompt, c_sample, mix_mod_w, mix_mod_b, mix_ln_g, mix_ln_b, ffn_mod_w, ffn_mod_b, ffn_ln_g, ffn_ln_b, w_in0, conv_w, conv_b, lru_wa, lru_ba, lru_wx, lru_bx, lru_lambda, sgu_ln_g, sgu_ln_b, sgu_ws, sgu_bs, w_out0, ffn_w_gate, ffn_w_up, ffn_w_down, w_qkv, lambda_q1, lambda_k1, lambda_q2, lambda_k2, subln_g, w_o, w_router, moe_w_gate, moe_w_up, moe_w_down):
    past = page_table.shape[1] * PAGE
    xp = x_prompt.reshape(N_PROMPT * SEQ, D)
    xs = x_sample.reshape(N_SAMPLE, D)
    vec = lambda a: a.reshape(1, -1)

    cond = jnp.concatenate([c_sample, c_prompt], axis=0)
    mix_mod = _cond_proj(cond, mix_mod_w, mix_mod_b)
    ffn_mod = _cond_proj(cond, ffn_mod_w, ffn_mod_b)

    w_in = w_in0[0].astype(BF16)
    w_out = w_out0[0].astype(BF16)
    wa, wx = _block_diag(lru_wa[0]), _block_diag(lru_wx[0])
    lru = (conv_w[0], vec(conv_b[0]), wa, vec(lru_ba[0]), wx, vec(lru_bx[0]), vec(lru_lambda[0]))
    sgu_g, sgu_b = vec(sgu_ln_g[0]), vec(sgu_ln_b[0])
    ln0 = (vec(mix_ln_g[0]), vec(mix_ln_b[0]))

    xr_p, ggr_p, sgu_p = _mix0_in(xp, mix_mod, w_in, sgu_g, sgu_b, sgu_ws[0], sgu_bs[0].T)
    xp, ht_p = _mix0_rnn(xr_p, ggr_p, sgu_p, xp, mix_mod, *lru, w_out, *ln0)

    ws0 = vec(jnp.repeat(sgu_ws[0][:, 0, 0], GROUP_W))
    bs0 = vec(jnp.repeat(sgu_bs[0][:, 0], GROUP_W))
    state = jnp.swapaxes(state_rglru_conv[0], 0, 1)
    xs, xr_s, ht_s, vn_s = _mix0_sample(xs, mix_mod, w_in, state, state_rglru_h[0], *lru,
                                        sgu_g, sgu_b, ws0, bs0, w_out, *ln0)

    ffn_w = (ffn_w_gate[0].astype(BF16), ffn_w_up[0].astype(BF16), ffn_w_down[0].astype(BF16),
             vec(ffn_ln_g[0]), vec(ffn_ln_b[0]))
    xp = _ffn(xp, ffn_mod, True, *ffn_w, tm=512)
    xs = _ffn(xs, ffn_mod, False, *ffn_w, tm=N_SAMPLE)

    wqkv = w_qkv[0].astype(BF16)
    lp = jnp.stack([lambda_q1[0], lambda_k1[0], lambda_q2[0], lambda_k2[0]])
    cos_p, sin_p = _rope_tables(jnp.arange(SEQ))
    cos_s, sin_s = _rope_tables(jnp.full((1,), past))
    q_p, k_p, v_p = _qkv(xp, mix_mod, True, wqkv, cos_p, sin_p, tm=512)
    q_s, k_s, v_s = _qkv(xs, mix_mod, False, wqkv, cos_s, sin_s, tm=N_SAMPLE)
    o_p = _flash(lp, q_p, k_p, v_p)
    o_s = _decode(lp, q_s, k_s, v_s, cache_k, cache_v, page_table)
    att_w = (vec(subln_g[0]), w_o[0].astype(BF16), vec(mix_ln_g[1]), vec(mix_ln_b[1]))
    xp = _attn_out(o_p, xp, mix_mod, True, *att_w, tm=512)
    xs = _attn_out(o_s, xs, mix_mod, False, *att_w, tm=N_SAMPLE)

    xp, xs = _moe(xp, xs, ffn_mod, w_router[0], moe_w_gate[0], moe_w_up[0], moe_w_down[0],
                  vec(ffn_ln_g[1]), vec(ffn_ln_b[1]))

    new_conv_s = jnp.concatenate([state_rglru_conv[0][:, 1:], xr_s[:, None]], axis=1)
    return (xp.reshape(N_PROMPT, SEQ, D), xs.reshape(N_SAMPLE, 1, D),
            xr_p.reshape(N_PROMPT, SEQ, RNN_W)[:, SEQ - (CONV_W - 1):][None],
            ht_p.reshape(1, N_PROMPT, RNN_W),
            new_conv_s[None], ht_s[None], vn_s.reshape(1, N_SAMPLE, 1, SGU_W),
            k_p.reshape(1, N_PROMPT, SEQ, N_HEADS, HEAD_W), v_p.reshape(1, N_PROMPT, SEQ, N_HEADS, HEAD_W),
            k_s.reshape(1, N_SAMPLE, 1, N_HEADS, HEAD_W), v_s.reshape(1, N_SAMPLE, 1, N_HEADS, HEAD_W))
```

```python
import functools
import math

import jax
import jax.numpy as jnp
from jax import lax
from jax.experimental import pallas as pl
from jax.experimental.pallas import tpu as pltpu

F32 = jnp.float32
BF16 = jnp.bfloat16

D = 1024
SEQ = 2048
N_PROMPT = 8
N_SAMPLE = 32
N_COND = N_PROMPT + N_SAMPLE
PAGE = 128
RNN_W = 512
RNN_HALF = 256
SGU_W = 512
SGU_GROUPS = 4
GROUP_W = 128
CHUNK = 128
CONV_W = 4
LRU_C = 8.0
N_HEADS = 8
HEAD_W = 128
MAP_W = 64
ROT = 16
ROPE_THETA = 500000.0
D_FF = 2816
N_EXPERTS = 8
D_FF_E = 3584
LN_EPS = 1e-5
ALPHA = (2 * 2) ** 0.25
LAM_INIT = 0.8 - 0.6 * math.exp(-0.3 * 1)
ATT_SCALE = MAP_W ** -0.5
LANES = 128
NEG_INF = float("-inf")

VMEM_LIMIT = 56 * 1024 * 1024


def _params(sem):
    return pltpu.CompilerParams(dimension_semantics=sem, vmem_limit_bytes=VMEM_LIMIT)


def _dot(a, b):
    return jnp.dot(a, b, preferred_element_type=F32)


def _dot_nt(a, b):
    return lax.dot_general(a, b, (((1,), (1,)), ((), ())), preferred_element_type=F32)


def _layernorm(y, g, b):
    mu = jnp.mean(y, axis=-1, keepdims=True)
    yc = y - mu
    var = jnp.mean(yc * yc, axis=-1, keepdims=True)
    return yc * lax.rsqrt(var + LN_EPS) * g + b


def _silu(x):
    return x * jax.nn.sigmoid(x)


def _resident(shape):
    n = len(shape)
    return pl.BlockSpec(shape, lambda *_: (0,) * n, pipeline_mode=pl.Buffered(1))


def _row_spec(tm, width):
    return pl.BlockSpec((tm, width), lambda i: (i, 0))


def _mod_specs(mod, layer, parts, prompt, tm):
    if prompt:
        arr = mod.reshape(2, N_COND, 1, 3 * D)
        specs = [pl.BlockSpec((None, None, 1, D),
                              lambda i, *_, p=p: (layer, N_SAMPLE + (i * tm) // SEQ, 0, p)) for p in parts]
    else:
        arr = mod
        specs = [pl.BlockSpec((None, N_SAMPLE, D), lambda i, *_, p=p: (layer, 0, p)) for p in parts]
    return [arr] * len(parts), specs


def _cond_kernel(c_ref, w_ref, b_ref, o_ref):
    a = _silu(c_ref[...]).astype(BF16)
    o_ref[...] = _dot(a, w_ref[...].astype(BF16)) + b_ref[...]


def _cond_proj(c, w, b):
    tn = 512
    return pl.pallas_call(
        _cond_kernel,
        grid=(2, 3 * D // tn),
        in_specs=[pl.BlockSpec((N_COND, D), lambda l, j: (0, 0)),
                  pl.BlockSpec((None, D, tn), lambda l, j: (l, 0, j)),
                  pl.BlockSpec((None, 1, tn), lambda l, j: (l, 0, j))],
        out_specs=pl.BlockSpec((None, N_COND, tn), lambda l, j: (l, 0, j)),
        out_shape=jax.ShapeDtypeStruct((2, N_COND, 3 * D), F32),
        compiler_params=_params(("arbitrary", "arbitrary")),
        name="cond_proj",
    )(c, w, b.reshape(2, 1, 3 * D))


def _group_layernorm(v, g, b):
    outs = []
    for k in range(SGU_GROUPS):
        sl = slice(k * GROUP_W, (k + 1) * GROUP_W)
        outs.append(_layernorm(v[:, sl], g[:, sl], b[:, sl]))
    return outs


def _mix0_in_kernel(x_ref, sh_ref, sc_ref, w_ref, lng_ref, lnb_ref, ws_ref, bst_ref,
                    xr_ref, ggr_ref, sgu_ref, *, tm):
    h = (x_ref[...] * (1.0 + sc_ref[...]) + sh_ref[...]).astype(BF16)
    xr_ref[...] = _dot(h, w_ref[:, 0:RNN_W])
    ggr_ref[...] = jax.nn.gelu(_dot(h, w_ref[:, RNN_W:2 * RNN_W]))
    gu = jax.nn.gelu(_dot(h, w_ref[:, 2 * RNN_W:2 * RNN_W + SGU_W]))
    gv = jax.nn.gelu(_dot(h, w_ref[:, 2 * RNN_W + SGU_W:]))
    vn = _group_layernorm(gv, lng_ref[...], lnb_ref[...])
    row = lax.broadcasted_iota(jnp.int32, (CHUNK, CHUNK), 0)
    col = lax.broadcasted_iota(jnp.int32, (CHUNK, CHUNK), 1)
    for k in range(SGU_GROUPS):
        ws = jnp.where(col <= row, ws_ref[k], 0.0).astype(BF16)
        vk = vn[k].astype(BF16)
        bias = bst_ref[:, k:k + 1]
        for c in range(tm // CHUNK):
            rows = slice(c * CHUNK, (c + 1) * CHUNK)
            cols = slice(k * GROUP_W, (k + 1) * GROUP_W)
            s = _dot(ws, vk[rows]) + bias
            sgu_ref[rows, cols] = gu[rows, cols] * s


def _mix0_in(x, mod, w_in, ln_g, ln_b, ws, bs_t, tm=512):
    t = x.shape[0]
    mod_args, mod_specs = _mod_specs(mod, 0, (0, 1), True, tm)
    out = jax.ShapeDtypeStruct((t, RNN_W), F32)
    return pl.pallas_call(
        functools.partial(_mix0_in_kernel, tm=tm),
        grid=(t // tm,),
        in_specs=[_row_spec(tm, D), *mod_specs, _resident((D, 4 * RNN_W)),
                  _resident((1, SGU_W)), _resident((1, SGU_W)),
                  _resident((SGU_GROUPS, CHUNK, CHUNK)), _resident((CHUNK, SGU_GROUPS))],
        out_specs=[_row_spec(tm, RNN_W)] * 3,
        out_shape=[out] * 3,
        compiler_params=_params(("arbitrary",)),
        name="mix0_in",
    )(x, *mod_args, w_in, ln_g, ln_b, ws, bs_t)


def _lru_gates(xc, wa_ref, ba, wx_ref, bx, lam):
    xb = xc.astype(BF16)
    lo, hi = xb[:, :RNN_HALF], xb[:, RNN_HALF:]
    r = jax.nn.sigmoid(jnp.concatenate([_dot(lo, wa_ref[0]), _dot(hi, wa_ref[1])], axis=1) + ba)
    i = jax.nn.sigmoid(jnp.concatenate([_dot(lo, wx_ref[0]), _dot(hi, wx_ref[1])], axis=1) + bx)
    log_a = (-LRU_C) * r * jax.nn.softplus(-lam)
    a = jnp.exp(log_a)
    u = jnp.sqrt(-jnp.tanh(log_a) * (a * a + 1.0)) * (i * xc)
    return a, u


def _mix0_rnn_kernel(xr_ref, ggr_ref, sgu_ref, x_ref, gt_ref, cw_ref, cb_ref, wa_ref, ba_ref,
                     wx_ref, bx_ref, lam_ref, wo_ref, g_ref, b_ref,
                     y_ref, ht_ref, buf_ref, a_ref, u_ref, hs_ref, hc_ref, *, ts):
    t = pl.program_id(1)

    @pl.when(t == 0)
    def _():
        buf_ref[0:8, :] = jnp.zeros((8, RNN_W), F32)
        hc_ref[...] = jnp.zeros((1, RNN_W), F32)

    xr = xr_ref[...]
    buf_ref[8:8 + ts, :] = xr
    cw = cw_ref[...]
    xc = (buf_ref[5:5 + ts, :] * cw[0:1] + buf_ref[6:6 + ts, :] * cw[1:2]
          + buf_ref[7:7 + ts, :] * cw[2:3] + xr * cw[3:4]) + cb_ref[...]
    buf_ref[0:8, :] = xr[ts - 8:ts, :]

    a, u = _lru_gates(xc, wa_ref, ba_ref[...], wx_ref, bx_ref[...], lam_ref[...])
    a_ref[...] = a
    u_ref[...] = u

    row = lax.broadcasted_iota(jnp.int32, (8, RNN_W), 0)

    def tile(k, h_prev):
        off = pl.multiple_of(k * 8, 8)
        aa = a_ref[pl.ds(off, 8), :]
        uu = u_ref[pl.ds(off, 8), :]
        for d in (1, 2, 4):
            keep = row >= d
            a_sh = jnp.where(keep, pltpu.roll(aa, d, 0), 1.0)
            u_sh = jnp.where(keep, pltpu.roll(uu, d, 0), 0.0)
            uu = uu + aa * u_sh
            aa = aa * a_sh
        h = aa * h_prev + uu
        hs_ref[pl.ds(off, 8), :] = h
        return h[7:8, :]

    h_last = lax.fori_loop(0, ts // 8, tile, hc_ref[...], unroll=4)
    hc_ref[...] = h_last

    @pl.when(t == pl.num_programs(1) - 1)
    def _():
        ht_ref[...] = h_last

    rnn = (hs_ref[...] * ggr_ref[...]).astype(BF16)
    out = _dot(rnn, wo_ref[0:RNN_W, :]) + _dot(sgu_ref[...].astype(BF16), wo_ref[RNN_W:, :])
    y = ALPHA * x_ref[...] + (1.0 + gt_ref[...]) * out
    y_ref[...] = _layernorm(y, g_ref[...], b_ref[...])


def _mix0_rnn(xr, ggr, sgu, x, mod, conv_w, conv_b, wa, ba, wx, bx, lam, w_out, ln_g, ln_b, ts=512):
    nt = SEQ // ts
    arr = mod.reshape(2, N_COND, 1, 3 * D)
    tile_spec = lambda w: pl.BlockSpec((ts, w), lambda b, t: (b * nt + t, 0))
    res = lambda shape: pl.BlockSpec(shape, lambda b, t: (0,) * len(shape), pipeline_mode=pl.Buffered(1))
    return pl.pallas_call(
        functools.partial(_mix0_rnn_kernel, ts=ts),
        grid=(N_PROMPT, nt),
        in_specs=[tile_spec(RNN_W), tile_spec(RNN_W), tile_spec(SGU_W), tile_spec(D),
                  pl.BlockSpec((None, None, 1, D), lambda b, t: (0, N_SAMPLE + b, 0, 2)),
                  res((CONV_W, RNN_W)), res((1, RNN_W)),
                  res((2, RNN_HALF, RNN_HALF)), res((1, RNN_W)),
                  res((2, RNN_HALF, RNN_HALF)), res((1, RNN_W)), res((1, RNN_W)),
                  res((D, D)), res((1, D)), res((1, D))],
        out_specs=[tile_spec(D), pl.BlockSpec((None, 1, RNN_W), lambda b, t: (b, 0, 0))],
        out_shape=[jax.ShapeDtypeStruct((N_PROMPT * SEQ, D), F32),
                   jax.ShapeDtypeStruct((N_PROMPT, 1, RNN_W), F32)],
        scratch_shapes=[pltpu.VMEM((ts + 8, RNN_W), F32), pltpu.VMEM((ts, RNN_W), F32),
                        pltpu.VMEM((ts, RNN_W), F32), pltpu.VMEM((ts, RNN_W), F32),
                        pltpu.VMEM((1, RNN_W), F32)],
        compiler_params=_params(("arbitrary", "arbitrary")),
        name="mix0_rnn",
    )(xr, ggr, sgu, x, arr, conv_w, conv_b, wa, ba, wx, bx, lam, w_out, ln_g, ln_b)


def _mix0_sample_kernel(x_ref, sh_ref, sc_ref, gt_ref, w_ref, st_ref, h0_ref, cw_ref, cb_ref,
                        wa_ref, ba_ref, wx_ref, bx_ref, lam_ref, lng_ref, lnb_ref, ws0_ref, bs0_ref,
                        wo_ref, g_ref, b_ref, y_ref, xr_ref, ht_ref, vn_ref):
    x = x_ref[...]
    h = (x * (1.0 + sc_ref[...]) + sh_ref[...]).astype(BF16)
    xr = _dot(h, w_ref[:, 0:RNN_W])
    ggr = jax.nn.gelu(_dot(h, w_ref[:, RNN_W:2 * RNN_W]))
    gu = jax.nn.gelu(_dot(h, w_ref[:, 2 * RNN_W:2 * RNN_W + SGU_W]))
    gv = jax.nn.gelu(_dot(h, w_ref[:, 2 * RNN_W + SGU_W:]))
    xr_ref[...] = xr
    cw = cw_ref[...]
    xc = (st_ref[0] * cw[0:1] + st_ref[1] * cw[1:2] + st_ref[2] * cw[2:3] + xr * cw[3:4]) + cb_ref[...]
    a, u = _lru_gates(xc, wa_ref, ba_ref[...], wx_ref, bx_ref[...], lam_ref[...])
    h_new = a * h0_ref[...] + u
    ht_ref[...] = h_new
    vn = jnp.concatenate(_group_layernorm(gv, lng_ref[...], lnb_ref[...]), axis=1)
    vn_ref[...] = vn
    sgu = gu * (ws0_ref[...] * vn + bs0_ref[...])
    rnn = (h_new * ggr).astype(BF16)
    out = _dot(rnn, wo_ref[0:RNN_W, :]) + _dot(sgu.astype(BF16), wo_ref[RNN_W:, :])
    y = ALPHA * x + (1.0 + gt_ref[...]) * out
    y_ref[...] = _layernorm(y, g_ref[...], b_ref[...])


def _mix0_sample(x, mod, w_in, state, h0, conv_w, conv_b, wa, ba, wx, bx, lam, lng, lnb, ws0, bs0,
                 w_out, ln_g, ln_b):
    mod_args, mod_specs = _mod_specs(mod, 0, (0, 1, 2), False, N_SAMPLE)
    full = lambda a: pl.BlockSpec(a.shape, lambda i: (0,) * a.ndim)
    rest = (w_in, state, h0, conv_w, conv_b, wa, ba, wx, bx, lam, lng, lnb, ws0, bs0, w_out, ln_g, ln_b)
    vec = jax.ShapeDtypeStruct((N_SAMPLE, RNN_W), F32)
    return pl.pallas_call(
        _mix0_sample_kernel,
        grid=(1,),
        in_specs=[full(x), *mod_specs, *[full(a) for a in rest]],
        out_specs=[_row_spec(N_SAMPLE, D)] + [_row_spec(N_SAMPLE, RNN_W)] * 3,
        out_shape=[jax.ShapeDtypeStruct((N_SAMPLE, D), F32), vec, vec, vec],
        compiler_params=_params(("arbitrary",)),
        name="mix0_sample",
    )(x, *mod_args, *rest)


def _ffn_kernel(x_ref, sh_ref, sc_ref, gt_ref, wg_ref, wu_ref, wd_ref, g_ref, b_ref, y_ref, acc_ref,
                *, bounds):
    x = x_ref[...]
    h = (x * (1.0 + sc_ref[...]) + sh_ref[...]).astype(BF16)
    for n, (lo, hi) in enumerate(bounds):
        act = (_silu(_dot(h, wg_ref[:, lo:hi])) * _dot(h, wu_ref[:, lo:hi])).astype(BF16)
        part = _dot(act, wd_ref[lo:hi, :])
        if n == 0:
            acc_ref[...] = part
        else:
            acc_ref[...] += part
    y = ALPHA * x + (1.0 + gt_ref[...]) * acc_ref[...]
    y_ref[...] = _layernorm(y, g_ref[...], b_ref[...])


def _ffn(x, mod, prompt, wg, wu, wd, ln_g, ln_b, tm):
    t = x.shape[0]
    step = 512
    bounds = tuple((lo, min(lo + step, D_FF)) for lo in range(0, D_FF, step))
    mod_args, mod_specs = _mod_specs(mod, 0, (0, 1, 2), prompt, tm)
    return pl.pallas_call(
        functools.partial(_ffn_kernel, bounds=bounds),
        grid=(t // tm,),
        in_specs=[_row_spec(tm, D), *mod_specs, _resident((D, D_FF)), _resident((D, D_FF)),
                  _resident((D_FF, D)), _resident((1, D)), _resident((1, D))],
        out_specs=_row_spec(tm, D),
        out_shape=jax.ShapeDtypeStruct((t, D), F32),
        scratch_shapes=[pltpu.VMEM((tm, D), F32)],
        compiler_params=_params(("arbitrary",)),
        name="ffn_prompt" if prompt else "ffn_sample",
    )(x, *mod_args, wg, wu, wd, ln_g, ln_b)


def _rope(y, cos, sin_signed):
    lane = lax.broadcasted_iota(jnp.int32, y.shape, 1)
    first_half = (lane % MAP_W) < (ROT // 2)
    partner = jnp.where(first_half, pltpu.roll(y, HEAD_W - ROT // 2, 1), pltpu.roll(y, ROT // 2, 1))
    return y * cos + partner * sin_signed


def _qkv_kernel(x_ref, sh_ref, sc_ref, w_ref, cos_ref, sin_ref, q_ref, k_ref, v_ref):
    h = (x_ref[...] * (1.0 + sc_ref[...]) + sh_ref[...]).astype(BF16)
    cos, sin = cos_ref[...], sin_ref[...]
    width = 4 * HEAD_W
    for out_ref, base, rotate in ((q_ref, 0, True), (k_ref, D, True), (v_ref, 2 * D, False)):
        for j in range(D // width):
            y = _dot(h, w_ref[:, base + j * width:base + (j + 1) * width])
            if rotate:
                y = jnp.concatenate([_rope(y[:, n * HEAD_W:(n + 1) * HEAD_W], cos, sin)
                                     for n in range(width // HEAD_W)], axis=1)
            out_ref[:, j * width:(j + 1) * width] = y


def _qkv(x, mod, prompt, w_qkv, cos, sin, tm):
    t = x.shape[0]
    mod_args, mod_specs = _mod_specs(mod, 1, (0, 1), prompt, tm)
    if prompt:
        tab_spec = pl.BlockSpec((tm, HEAD_W), lambda i: (i % (SEQ // tm), 0))
    else:
        tab_spec = pl.BlockSpec((1, HEAD_W), lambda i: (0, 0))
    out = jax.ShapeDtypeStruct((t, D), F32)
    return pl.pallas_call(
        _qkv_kernel,
        grid=(t // tm,),
        in_specs=[_row_spec(tm, D), *mod_specs, _resident((D, 3 * D)), tab_spec, tab_spec],
        out_specs=[_row_spec(tm, D)] * 3,
        out_shape=[out] * 3,
        compiler_params=_params(("arbitrary",)),
        name="qkv_prompt" if prompt else "qkv_sample",
    )(x, *mod_args, w_qkv, cos, sin)


def _rope_tables(pos):
    half = ROT // 2
    inv = ROPE_THETA ** (-(jnp.arange(half, dtype=F32) * 2.0 / ROT))
    ang = pos.astype(F32)[:, None] * inv
    ones = jnp.ones((pos.shape[0], MAP_W - ROT), F32)
    cos = jnp.concatenate([jnp.cos(ang), jnp.cos(ang), ones], axis=1)
    sin = jnp.concatenate([-jnp.sin(ang), jnp.sin(ang), 0.0 * ones], axis=1)
    return jnp.tile(cos, (1, 2)), jnp.tile(sin, (1, 2))


def _diff_lambda(lp):
    s1 = jnp.sum(lp[0:1] * lp[1:2], axis=-1, keepdims=True)
    s2 = jnp.sum(lp[2:3] * lp[3:4], axis=-1, keepdims=True)
    return jnp.exp(s1) - jnp.exp(s2) + LAM_INIT


def _flash_kernel(lp_ref, q_ref, k_ref, v_ref, o_ref, kb_ref, vb_ref, qs_ref, m_ref, acc_ref, *, tq):
    qi = pl.program_id(2)

    @pl.when(qi == 0)
    def _():
        kb_ref[...] = k_ref[...].astype(BF16)
        vb_ref[:, 0:HEAD_W] = v_ref[...].astype(BF16)
        vb_ref[:, HEAD_W:] = jnp.ones((SEQ, HEAD_W), BF16)

    q = q_ref[...] * ATT_SCALE
    lane = lax.broadcasted_iota(jnp.int32, q.shape, 1)
    qs_ref[0] = jnp.where(lane < MAP_W, q, 0.0).astype(BF16)
    qs_ref[1] = jnp.where(lane >= MAP_W, q, 0.0).astype(BF16)
    m_ref[...] = jnp.full(m_ref.shape, NEG_INF, F32)
    acc_ref[...] = jnp.zeros(acc_ref.shape, F32)

    def block(j, diagonal):
        off = pl.multiple_of(j * tq, tq)
        kj = kb_ref[pl.ds(off, tq), :]
        vj = vb_ref[pl.ds(off, tq), :]
        if diagonal:
            row = lax.broadcasted_iota(jnp.int32, (tq, tq), 0)
            col = lax.broadcasted_iota(jnp.int32, (tq, tq), 1)
            visible = col <= row
        for m in range(2):
            s = _dot_nt(qs_ref[m], kj)
            if diagonal:
                s = jnp.where(visible, s, NEG_INF)
            m_prev = m_ref[m]
            m_new = jnp.maximum(m_prev, jnp.max(s, axis=-1, keepdims=True))
            p = jnp.exp(s - m_new).astype(BF16)
            acc_ref[m] = jnp.exp(m_prev - m_new) * acc_ref[m] + _dot(p, vj)
            m_ref[m] = m_new

    def below_diagonal(j, carry):
        block(j, False)
        return carry

    lax.fori_loop(0, qi, below_diagonal, 0)
    block(qi, True)
    o1 = acc_ref[0][:, 0:HEAD_W] / acc_ref[0][:, HEAD_W:]
    o2 = acc_ref[1][:, 0:HEAD_W] / acc_ref[1][:, HEAD_W:]
    o_ref[...] = o1 - _diff_lambda(lp_ref[...]) * o2


def _flash(lp, q, k, v, tq=512):
    nq = SEQ // tq
    q3, k3, v3 = (a.reshape(N_PROMPT, SEQ, D) for a in (q, k, v))
    q_spec = pl.BlockSpec((None, tq, HEAD_W), lambda b, h, i: (b, i, h))
    kv_spec = pl.BlockSpec((None, SEQ, HEAD_W), lambda b, h, i: (b, 0, h))
    return pl.pallas_call(
        functools.partial(_flash_kernel, tq=tq),
        grid=(N_PROMPT, N_HEADS, nq),
        in_specs=[pl.BlockSpec((4, MAP_W), lambda b, h, i: (0, 0)), q_spec, kv_spec, kv_spec],
        out_specs=q_spec,
        out_shape=jax.ShapeDtypeStruct((N_PROMPT, SEQ, D), F32),
        scratch_shapes=[pltpu.VMEM((SEQ, HEAD_W), BF16), pltpu.VMEM((SEQ, 2 * HEAD_W), BF16),
                        pltpu.VMEM((2, tq, HEAD_W), BF16), pltpu.VMEM((2, tq, 1), F32),
                        pltpu.VMEM((2, tq, 2 * HEAD_W), F32)],
        compiler_params=_params(("arbitrary",) * 3),
        name="diff_attn_prompt",
    )(lp, q3, k3, v3).reshape(N_PROMPT * SEQ, D)


def _decode_kernel(pt_ref, lp_ref, q_ref, kn_ref, vn_ref, *refs, n_pg):
    kc_refs, vc_refs = refs[:n_pg], refs[n_pg:2 * n_pg]
    o_ref, qb_ref, m_ref, l_ref, acc_ref = refs[2 * n_pg:]
    j = pl.program_id(1)

    @pl.when(j == 0)
    def _():
        q = q_ref[...] * ATT_SCALE
        lane = lax.broadcasted_iota(jnp.int32, q.shape, 1)
        qb_ref[0:N_HEADS, :] = jnp.where(lane < MAP_W, q, 0.0)
        qb_ref[N_HEADS:, :] = jnp.where(lane >= MAP_W, q, 0.0)
        m_ref[...] = jnp.full(m_ref.shape, NEG_INF, F32)
        l_ref[...] = jnp.zeros(l_ref.shape, F32)
        acc_ref[...] = jnp.zeros(acc_ref.shape, F32)

    rows = PAGE * N_HEADS
    qb = qb_ref[...].astype(BF16)
    r = lax.broadcasted_iota(jnp.int32, (2 * N_HEADS, rows), 0)
    c = lax.broadcasted_iota(jnp.int32, (2 * N_HEADS, rows), 1)
    own_head = c % N_HEADS == r % N_HEADS
    scores = []
    for kc_ref in kc_refs:
        s = _dot_nt(qb, kc_ref[...].reshape(rows, HEAD_W).astype(BF16))
        scores.append(jnp.where(own_head, s, NEG_INF))
    m_prev = m_ref[...]
    m_new = m_prev
    for s in scores:
        m_new = jnp.maximum(m_new, jnp.max(s, axis=-1, keepdims=True))
    l_new = jnp.exp(m_prev - m_new) * l_ref[...]
    acc = jnp.exp(m_prev - m_new) * acc_ref[...]
    for s, vc_ref in zip(scores, vc_refs):
        p = jnp.exp(s - m_new)
        l_new = l_new + jnp.sum(p, axis=-1, keepdims=True)
        acc = acc + _dot(p.astype(BF16), vc_ref[...].reshape(rows, HEAD_W).astype(BF16))
    l_ref[...] = l_new
    acc_ref[...] = acc
    m_ref[...] = m_new

    @pl.when(j == pl.num_programs(1) - 1)
    def _():
        k_new = jnp.concatenate([kn_ref[...], kn_ref[...]], axis=0)
        v_new = jnp.concatenate([vn_ref[...], vn_ref[...]], axis=0)
        s_new = jnp.sum(qb_ref[...] * k_new, axis=-1, keepdims=True)
        m_old = m_ref[...]
        m_fin = jnp.maximum(m_old, s_new)
        a_old = jnp.exp(m_old - m_fin)
        p_new = jnp.exp(s_new - m_fin)
        l_fin = a_old * l_ref[...] + p_new
        o_maps = (a_old * acc_ref[...] + p_new * v_new) / l_fin
        o_ref[...] = o_maps[0:N_HEADS] - _diff_lambda(lp_ref[...]) * o_maps[N_HEADS:]


def _decode(lp, q, k_new, v_new, cache_k, cache_v, page_table, n_pg=8):
    n_pages = page_table.shape[1]
    assert n_pages % n_pg == 0
    kc = cache_k.reshape(-1, PAGE, N_HEADS, HEAD_W)
    vc = cache_v.reshape(-1, PAGE, N_HEADS, HEAD_W)
    heads = lambda a: a.reshape(N_SAMPLE, N_HEADS, HEAD_W)
    head_spec = pl.BlockSpec((None, N_HEADS, HEAD_W), lambda b, j, pt: (b, 0, 0))
    page_specs = [pl.BlockSpec((None, PAGE, N_HEADS, HEAD_W),
                               lambda b, j, pt, g=g: (pt[b, j * n_pg + g], 0, 0, 0)) for g in range(n_pg)]
    n_maps = 2 * N_HEADS
    return pl.pallas_call(
        functools.partial(_decode_kernel, n_pg=n_pg),
        grid_spec=pltpu.PrefetchScalarGridSpec(
            num_scalar_prefetch=1,
            grid=(N_SAMPLE, n_pages // n_pg),
            in_specs=[pl.BlockSpec((4, MAP_W), lambda b, j, pt: (0, 0)), head_spec, head_spec, head_spec,
                      *page_specs, *page_specs],
            out_specs=head_spec,
            scratch_shapes=[pltpu.VMEM((n_maps, HEAD_W), F32), pltpu.VMEM((n_maps, 1), F32),
                            pltpu.VMEM((n_maps, 1), F32), pltpu.VMEM((n_maps, HEAD_W), F32)]),
        out_shape=jax.ShapeDtypeStruct((N_SAMPLE, N_HEADS, HEAD_W), F32),
        compiler_params=_params(("arbitrary", "arbitrary")),
        name="diff_attn_decode",
    )(page_table, lp, heads(q), heads(k_new), heads(v_new), *[kc] * n_pg, *[vc] * n_pg).reshape(N_SAMPLE, D)


def _attn_out_kernel(o_ref, x_ref, gt_ref, sg_ref, wo_ref, g_ref, b_ref, y_ref):
    o = o_ref[...]
    heads = []
    for n in range(N_HEADS):
        oh = o[:, n * HEAD_W:(n + 1) * HEAD_W]
        ms = jnp.mean(oh * oh, axis=-1, keepdims=True)
        heads.append((oh * lax.rsqrt(ms + LN_EPS) * sg_ref[...]) * (1.0 - LAM_INIT))
    out = _dot(jnp.concatenate(heads, axis=1).astype(BF16), wo_ref[...])
    y = ALPHA * x_ref[...] + (1.0 + gt_ref[...]) * out
    y_ref[...] = _layernorm(y, g_ref[...], b_ref[...])


def _attn_out(o, x, mod, prompt, subln_g, w_o, ln_g, ln_b, tm):
    t = x.shape[0]
    mod_args, mod_specs = _mod_specs(mod, 1, (2,), prompt, tm)
    return pl.pallas_call(
        _attn_out_kernel,
        grid=(t // tm,),
        in_specs=[_row_spec(tm, D), _row_spec(tm, D), *mod_specs, _resident((1, HEAD_W)),
                  _resident((D, D)), _resident((1, D)), _resident((1, D))],
        out_specs=_row_spec(tm, D),
        out_shape=jax.ShapeDtypeStruct((t, D), F32),
        compiler_params=_params(("arbitrary",)),
        name="attn_out_prompt" if prompt else "attn_out_sample",
    )(o, x, *mod_args, subln_g, w_o, ln_g, ln_b)


ROUTE_IDX = N_EXPERTS
ROUTE_GATE = N_EXPERTS + 2
MOE_TM = 1024
MOE_HALF = MOE_TM // 2
MOE_TF = 512
ROW_TILE = 256


def _route(logits):
    lane = lax.broadcasted_iota(jnp.int32, logits.shape, 1)
    z = jnp.where(lane < N_EXPERTS, logits, NEG_INF)
    m1 = jnp.max(z, axis=-1, keepdims=True)
    i1 = jnp.min(jnp.where(z == m1, lane, LANES), axis=-1, keepdims=True)
    z2 = jnp.where(lane == i1, NEG_INF, z)
    m2 = jnp.max(z2, axis=-1, keepdims=True)
    i2 = jnp.min(jnp.where(z2 == m2, lane, LANES), axis=-1, keepdims=True)
    e2 = jnp.exp(m2 - m1)
    g1 = 1.0 / (1.0 + e2)
    g2 = e2 / (1.0 + e2)
    out = jnp.where(lane == ROUTE_IDX, i1.astype(F32), 0.0) + jnp.where(lane == ROUTE_IDX + 1, i2.astype(F32), 0.0)
    return out + jnp.where(lane == ROUTE_GATE, g1, 0.0) + jnp.where(lane == ROUTE_GATE + 1, g2, 0.0)


def _router_kernel(x_ref, sh_ref, sc_ref, wr_ref, r_ref):
    h = (x_ref[...] * (1.0 + sc_ref[...]) + sh_ref[...]).astype(BF16)
    r_ref[...] = _route(_dot(h, wr_ref[...].astype(BF16)))


def _router(x, mod, prompt, w_router, tm):
    t = x.shape[0]
    arrs, specs = _mod_specs(mod, 1, (0, 1), prompt, tm)
    return pl.pallas_call(
        _router_kernel,
        grid=(t // tm,),
        in_specs=[_row_spec(tm, D), *specs, _resident((D, LANES))],
        out_specs=_row_spec(tm, LANES),
        out_shape=jax.ShapeDtypeStruct((t, LANES), F32),
        compiler_params=_params(("arbitrary",)),
        name="router_prompt" if prompt else "router_sample",
    )(x, *arrs, w_router)


def _moe_plan(expert_ids, tm, n_tiles):
    onehot = (expert_ids[:, None] == jnp.arange(N_EXPERTS, dtype=jnp.int32)[None, :]).astype(jnp.int32)
    rank = jnp.cumsum(onehot, axis=0) - onehot
    counts = jnp.sum(onehot, axis=0)
    tiles_per = (counts + tm - 1) // tm
    tile_end = jnp.cumsum(tiles_per)
    tile_start = tile_end - tiles_per
    pos = jnp.sum(onehot * (tile_start * tm + rank), axis=1)
    n_used = tile_end[-1]
    tile = jnp.minimum(jnp.arange(n_tiles, dtype=jnp.int32), n_used - 1)
    expert = jnp.sum((tile[:, None] >= tile_end[None, :]).astype(jnp.int32), axis=1)
    rows = jnp.clip(counts[expert] - (tile - tile_start[expert]) * tm, 0, tm)
    return pos.astype(jnp.int32), tile, expert.astype(jnp.int32), rows.astype(jnp.int32), n_used.reshape(1)


def _token_rows(x):
    return x.reshape(x.shape[0], D // LANES, LANES)


def _wait_tile_of_rows(src_ref, dst_ref, sem, choices=2):
    for _ in range(choices):
        pltpu.make_async_copy(src_ref, dst_ref, sem).wait()


def _dispatch_kernel(pos_ref, x_ref, sh_ref, sc_ref, xg_in_ref, xg_ref, h_ref, sem, *, tr, n_tok):
    del xg_in_ref
    i = pl.program_id(0)
    n = pl.num_programs(0)
    slot = i % 2
    sized = xg_ref.at[pl.ds(0, tr)]

    @pl.when(i >= 2)
    def _():
        _wait_tile_of_rows(h_ref.at[slot], sized, sem.at[slot])

    h_ref[slot] = _token_rows(x_ref[...] * (1.0 + sc_ref[...]) + sh_ref[...])

    def send(r, carry):
        for k in range(2):
            dst = pos_ref[k * n_tok + i * tr + r]
            pltpu.make_async_copy(h_ref.at[slot, r], xg_ref.at[dst], sem.at[slot]).start()
        return carry

    lax.fori_loop(0, tr, send, 0, unroll=8)

    @pl.when(i == n - 1)
    def _():
        @pl.when(i >= 1)
        def _():
            _wait_tile_of_rows(h_ref.at[1 - slot], sized, sem.at[1 - slot])
        _wait_tile_of_rows(h_ref.at[slot], sized, sem.at[slot])


def _dispatch(x, mod, prompt, pos, xg, tr):
    t = x.shape[0]
    arrs, specs = _mod_specs(mod, 1, (0, 1), prompt, tr)
    any_spec = pl.BlockSpec(memory_space=pl.ANY)
    return pl.pallas_call(
        functools.partial(_dispatch_kernel, tr=tr, n_tok=t),
        grid_spec=pltpu.PrefetchScalarGridSpec(
            num_scalar_prefetch=1,
            grid=(t // tr,),
            in_specs=[pl.BlockSpec((tr, D), lambda i, pos: (i, 0)), *specs, any_spec],
            out_specs=any_spec,
            scratch_shapes=[pltpu.VMEM((2, tr, D // LANES, LANES), F32), pltpu.SemaphoreType.DMA((2,))]),
        out_shape=jax.ShapeDtypeStruct(xg.shape, F32),
        input_output_aliases={4: 0},
        compiler_params=_params(("arbitrary",)),
        name="moe_dispatch_prompt" if prompt else "moe_dispatch_sample",
    )(pos, x, *arrs, xg)


def _expert_kernel(tile_ref, exp_ref, rows_ref, used_ref, xg_ref, wg_ref, wu_ref, wd_ref, yg_ref,
                   hb_ref, acc_ref, wgb_ref, wub_ref, wdb_ref):
    i = pl.program_id(0)
    f = pl.program_id(1)
    last_f = pl.num_programs(1) - 1
    live = i < used_ref[0]

    @pl.when(live)
    def _():
        wgb_ref[...] = wg_ref[...].astype(BF16)
        wub_ref[...] = wu_ref[...].astype(BF16)
        wdb_ref[...] = wd_ref[...].astype(BF16)

    for half in range(MOE_TM // MOE_HALF):
        rows = slice(half * MOE_HALF, (half + 1) * MOE_HALF)
        occupied = live & (rows_ref[i] > half * MOE_HALF)

        @pl.when(occupied)
        def _():
            @pl.when(f == 0)
            def _():
                hb_ref[rows, :] = xg_ref[rows].reshape(MOE_HALF, D).astype(BF16)

            h = hb_ref[rows, :]
            act = (_silu(_dot(h, wgb_ref[...])) * _dot(h, wub_ref[...])).astype(BF16)
            part = _dot(act, wdb_ref[...])

            @pl.when(f == 0)
            def _():
                acc_ref[rows, :] = part

            @pl.when(f > 0)
            def _():
                acc_ref[rows, :] += part

            @pl.when(f == last_f)
            def _():
                yg_ref[rows] = _token_rows(acc_ref[rows, :])

        @pl.when(jnp.logical_not(occupied) & (f == 0))
        def _():
            yg_ref[rows] = jnp.zeros((MOE_HALF, D // LANES, LANES), F32)


def _experts(xg, tile, expert, rows, n_used, wg, wu, wd):
    n_tiles = xg.shape[0] // MOE_TM
    nf = D_FF_E // MOE_TF

    def f_block(i, f, used):
        return jnp.where(i < used[0], f, nf - 1)

    return pl.pallas_call(
        _expert_kernel,
        grid_spec=pltpu.PrefetchScalarGridSpec(
            num_scalar_prefetch=4,
            grid=(n_tiles, nf),
            in_specs=[pl.BlockSpec((MOE_TM, D // LANES, LANES), lambda i, f, tile, exp, rows, used: (tile[i], 0, 0)),
                      pl.BlockSpec((None, D, MOE_TF),
                                   lambda i, f, tile, exp, rows, used: (exp[i], 0, f_block(i, f, used))),
                      pl.BlockSpec((None, D, MOE_TF),
                                   lambda i, f, tile, exp, rows, used: (exp[i], 0, f_block(i, f, used))),
                      pl.BlockSpec((None, MOE_TF, D),
                                   lambda i, f, tile, exp, rows, used: (exp[i], f_block(i, f, used), 0))],
            out_specs=pl.BlockSpec((MOE_TM, D // LANES, LANES), lambda i, f, tile, exp, rows, used: (i, 0, 0)),
            scratch_shapes=[pltpu.VMEM((MOE_TM, D), BF16), pltpu.VMEM((MOE_TM, D), F32),
                            pltpu.VMEM((D, MOE_TF), BF16), pltpu.VMEM((D, MOE_TF), BF16),
                            pltpu.VMEM((MOE_TF, D), BF16)]),
        out_shape=jax.ShapeDtypeStruct(xg.shape, F32),
        compiler_params=_params(("arbitrary", "arbitrary")),
        name="moe_experts",
    )(tile, expert, rows, n_used, xg, wg, wu, wd)


def _combine_kernel(pos_ref, yg_ref, r_ref, x_ref, gt_ref, g_ref, b_ref, y_ref, buf_ref, sem, *, tr, n_tok):
    i = pl.program_id(0)
    n = pl.num_programs(0)
    slot = i % 2

    def fetch(step, into):
        def one(r, carry):
            for k in range(2):
                src = pos_ref[k * n_tok + step * tr + r]
                pltpu.make_async_copy(yg_ref.at[src], buf_ref.at[into, k, r], sem.at[into]).start()
            return carry
        lax.fori_loop(0, tr, one, 0, unroll=8)

    @pl.when(i == 0)
    def _():
        fetch(0, 0)

    @pl.when(i + 1 < n)
    def _():
        fetch(i + 1, 1 - slot)

    _wait_tile_of_rows(yg_ref.at[pl.ds(0, tr)], buf_ref.at[slot, 0], sem.at[slot])
    route = r_ref[...]
    first = buf_ref[slot, 0].reshape(tr, D)
    second = buf_ref[slot, 1].reshape(tr, D)
    moe = route[:, ROUTE_GATE:ROUTE_GATE + 1] * first + route[:, ROUTE_GATE + 1:ROUTE_GATE + 2] * second
    y = ALPHA * x_ref[...] + (1.0 + gt_ref[...]) * moe
    y_ref[...] = _layernorm(y, g_ref[...], b_ref[...])


def _combine(yg, pos, route, x, mod, prompt, ln_g, ln_b, tr):
    t = x.shape[0]
    arrs, specs = _mod_specs(mod, 1, (2,), prompt, tr)
    return pl.pallas_call(
        functools.partial(_combine_kernel, tr=tr, n_tok=t),
        grid_spec=pltpu.PrefetchScalarGridSpec(
            num_scalar_prefetch=1,
            grid=(t // tr,),
            in_specs=[pl.BlockSpec(memory_space=pl.ANY),
                      pl.BlockSpec((tr, LANES), lambda i, pos: (i, 0)),
                      pl.BlockSpec((tr, D), lambda i, pos: (i, 0)), *specs,
                      pl.BlockSpec((1, D), lambda i, pos: (0, 0)), pl.BlockSpec((1, D), lambda i, pos: (0, 0))],
            out_specs=pl.BlockSpec((tr, D), lambda i, pos: (i, 0)),
            scratch_shapes=[pltpu.VMEM((2, 2, tr, D // LANES, LANES), F32), pltpu.SemaphoreType.DMA((2,))]),
        out_shape=jax.ShapeDtypeStruct((t, D), F32),
        compiler_params=_params(("arbitrary",)),
        name="moe_combine_prompt" if prompt else "moe_combine_sample",
    )(pos, yg, route, x, *arrs, ln_g, ln_b)


def _moe(xp, xs, mod, w_router, wg, wu, wd, ln_g, ln_b):
    tp, ts = xp.shape[0], xs.shape[0]
    w_r = jnp.pad(w_router, ((0, 0), (0, LANES - N_EXPERTS)))
    route_p = _router(xp, mod, True, w_r, tm=512)
    route_s = _router(xs, mod, False, w_r, tm=ts)
    ids = lambda r: r[:, ROUTE_IDX:ROUTE_IDX + 2].astype(jnp.int32).T.reshape(-1)
    n_pairs = 2 * (tp + ts)
    n_tiles = (n_pairs + N_EXPERTS * (MOE_TM - 1)) // MOE_TM
    pos, tile, expert, rows, n_used = _moe_plan(jnp.concatenate([ids(route_p), ids(route_s)]), MOE_TM, n_tiles)
    pos_p, pos_s = pos[:2 * tp], pos[2 * tp:]
    xg = jnp.zeros((n_tiles * MOE_TM, D // LANES, LANES), F32)
    xg = _dispatch(xp, mod, True, pos_p, xg, ROW_TILE)
    xg = _dispatch(xs, mod, False, pos_s, xg, ts)
    yg = _experts(xg, tile, expert, rows, n_used, wg, wu, wd)
    yp = _combine(yg, pos_p, route_p, xp, mod, True, ln_g, ln_b, ROW_TILE)
    ys = _combine(yg, pos_s, route_s, xs, mod, False, ln_g, ln_b, ts)
    return yp, ys


def _block_diag(w):
    per = RNN_HALF // w.shape[-1]
    halves = [jax.scipy.linalg.block_diag(*[w[n * per + j] for j in range(per)]) for n in range(2)]
    return jnp.stack(halves).astype(BF16)


def kernel(x_prompt, x_sample, state_rglru_conv, state_rglru_h, cache_k, cache_v, page_table, c_prompt, c_sample, mix_mod_w, mix_mod_b, mix_ln_g, mix_ln_b, ffn_mod_w, ffn_mod_b, ffn_ln_g, ffn_ln_b, w_in0, conv_w, conv_b, lru_wa, lru_ba, lru_wx, lru_bx, lru_lambda, sgu_ln_g, sgu_ln_b, sgu_ws, sgu_bs, w_out0, ffn_w_gate, ffn_w_up, ffn_w_down, w_qkv, lambda_q1, lambda_k1, lambda_q2, lambda_k2, subln_g, w_o, w_router, moe_w_gate, moe_w_up, moe_w_down):
    past = page_table.shape[1] * PAGE
    xp = x_prompt.reshape(N_PROMPT * SEQ, D)
    xs = x_sample.reshape(N_SAMPLE, D)
    vec = lambda a: a.reshape(1, -1)

    cond = jnp.concatenate([c_sample, c_prompt], axis=0)
    mix_mod = _cond_proj(cond, mix_mod_w, mix_mod_b)
    ffn_mod = _cond_proj(cond, ffn_mod_w, ffn_mod_b)

    w_in = w_in0[0].astype(BF16)
    w_out = w_out0[0].astype(BF16)
    wa, wx = _block_diag(lru_wa[0]), _block_diag(lru_wx[0])
    lru = (conv_w[0], vec(conv_b[0]), wa, vec(lru_ba[0]), wx, vec(lru_bx[0]), vec(lru_lambda[0]))
    sgu_g, sgu_b = vec(sgu_ln_g[0]), vec(sgu_ln_b[0])
    ln0 = (vec(mix_ln_g[0]), vec(mix_ln_b[0]))

    xr_p, ggr_p, sgu_p = _mix0_in(xp, mix_mod, w_in, sgu_g, sgu_b, sgu_ws[0], sgu_bs[0].T)
    xp, ht_p = _mix0_rnn(xr_p, ggr_p, sgu_p, xp, mix_mod, *lru, w_out, *ln0)

    ws0 = vec(jnp.repeat(sgu_ws[0][:, 0, 0], GROUP_W))
    bs0 = vec(jnp.repeat(sgu_bs[0][:, 0], GROUP_W))
    state = jnp.swapaxes(state_rglru_conv[0], 0, 1)
    xs, xr_s, ht_s, vn_s = _mix0_sample(xs, mix_mod, w_in, state, state_rglru_h[0], *lru,
                                        sgu_g, sgu_b, ws0, bs0, w_out, *ln0)

    ffn_w = (ffn_w_gate[0].astype(BF16), ffn_w_up[0].astype(BF16), ffn_w_down[0].astype(BF16),
             vec(ffn_ln_g[0]), vec(ffn_ln_b[0]))
    xp = _ffn(xp, ffn_mod, True, *ffn_w, tm=512)
    xs = _ffn(xs, ffn_mod, False, *ffn_w, tm=N_SAMPLE)

    wqkv = w_qkv[0].astype(BF16)
    lp = jnp.stack([lambda_q1[0], lambda_k1[0], lambda_q2[0], lambda_k2[0]])
    cos_p, sin_p = _rope_tables(jnp.arange(SEQ))
    cos_s, sin_s = _rope_tables(jnp.full((1,), past))
    q_p, k_p, v_p = _qkv(xp, mix_mod, True, wqkv, cos_p, sin_p, tm=512)
    q_s, k_s, v_s = _qkv(xs, mix_mod, False, wqkv, cos_s, sin_s, tm=N_SAMPLE)
    o_p = _flash(lp, q_p, k_p, v_p)
    o_s = _decode(lp, q_s, k_s, v_s, cache_k, cache_v, page_table)
    att_w = (vec(subln_g[0]), w_o[0].astype(BF16), vec(mix_ln_g[1]), vec(mix_ln_b[1]))
    xp = _attn_out(o_p, xp, mix_mod, True, *att_w, tm=512)
    xs = _attn_out(o_s, xs, mix_mod, False, *att_w, tm=N_SAMPLE)

    xp, xs = _moe(xp, xs, ffn_mod, w_router[0], moe_w_gate[0], moe_w_up[0], moe_w_down[0],
                  vec(ffn_ln_g[1]), vec(ffn_ln_b[1]))

    new_conv_s = jnp.concatenate([state_rglru_conv[0][:, 1:], xr_s[:, None]], axis=1)
    return (xp.reshape(N_PROMPT, SEQ, D), xs.reshape(N_SAMPLE, 1, D),
            xr_p.reshape(N_PROMPT, SEQ, RNN_W)[:, SEQ - (CONV_W - 1):][None],
            ht_p.reshape(1, N_PROMPT, RNN_W),
            new_conv_s[None], ht_s[None], vn_s.reshape(1, N_SAMPLE, 1, SGU_W),
            k_p.reshape(1, N_PROMPT, SEQ, N_HEADS, HEAD_W), v_p.reshape(1, N_PROMPT, SEQ, N_HEADS, HEAD_W),
            k_s.reshape(1, N_SAMPLE, 1, N_HEADS, HEAD_W), v_s.reshape(1, N_SAMPLE, 1, N_HEADS, HEAD_W))
```

```python
import functools
import math

import jax
import jax.numpy as jnp
from jax import lax
from jax.experimental import pallas as pl
from jax.experimental.pallas import tpu as pltpu

F32 = jnp.float32
BF16 = jnp.bfloat16

D = 1024
SEQ = 2048
N_PROMPT = 8
N_SAMPLE = 32
N_COND = N_PROMPT + N_SAMPLE
PAGE = 128
RNN_W = 512
RNN_HALF = 256
SGU_W = 512
SGU_GROUPS = 4
GROUP_W = 128
CHUNK = 128
CONV_W = 4
LRU_C = 8.0
N_HEADS = 8
HEAD_W = 128
MAP_W = 64
ROT = 16
ROPE_THETA = 500000.0
D_FF = 2816
N_EXPERTS = 8
D_FF_E = 3584
LN_EPS = 1e-5
ALPHA = (2 * 2) ** 0.25
LAM_INIT = 0.8 - 0.6 * math.exp(-0.3 * 1)
ATT_SCALE = MAP_W ** -0.5
LANES = 128
NEG_INF = float("-inf")

VMEM_LIMIT = 56 * 1024 * 1024


def _params(sem):
    return pltpu.CompilerParams(dimension_semantics=sem, vmem_limit_bytes=VMEM_LIMIT)


def _dot(a, b):
    return jnp.dot(a, b, preferred_element_type=F32)


def _dot_nt(a, b):
    return lax.dot_general(a, b, (((1,), (1,)), ((), ())), preferred_element_type=F32)


def _layernorm(y, g, b):
    mu = jnp.mean(y, axis=-1, keepdims=True)
    yc = y - mu
    var = jnp.mean(yc * yc, axis=-1, keepdims=True)
    return yc * lax.rsqrt(var + LN_EPS) * g + b


def _silu(x):
    return x * jax.nn.sigmoid(x)


def _resident(shape):
    n = len(shape)
    return pl.BlockSpec(shape, lambda *_: (0,) * n, pipeline_mode=pl.Buffered(1))


def _row_spec(tm, width):
    return pl.BlockSpec((tm, width), lambda i: (i, 0))


def _mod_specs(mod, layer, parts, prompt, tm):
    if prompt:
        arr = mod.reshape(2, N_COND, 1, 3 * D)
        specs = [pl.BlockSpec((None, None, 1, D),
                              lambda i, *_, p=p: (layer, N_SAMPLE + (i * tm) // SEQ, 0, p)) for p in parts]
    else:
        arr = mod
        specs = [pl.BlockSpec((None, N_SAMPLE, D), lambda i, *_, p=p: (layer, 0, p)) for p in parts]
    return [arr] * len(parts), specs


def _cond_kernel(c_ref, w_ref, b_ref, o_ref):
    a = _silu(c_ref[...]).astype(BF16)
    o_ref[...] = _dot(a, w_ref[...].astype(BF16)) + b_ref[...]


def _cond_proj(c, w, b):
    tn = 512
    return pl.pallas_call(
        _cond_kernel,
        grid=(2, 3 * D // tn),
        in_specs=[pl.BlockSpec((N_COND, D), lambda l, j: (0, 0)),
                  pl.BlockSpec((None, D, tn), lambda l, j: (l, 0, j)),
                  pl.BlockSpec((None, 1, tn), lambda l, j: (l, 0, j))],
        out_specs=pl.BlockSpec((None, N_COND, tn), lambda l, j: (l, 0, j)),
        out_shape=jax.ShapeDtypeStruct((2, N_COND, 3 * D), F32),
        compiler_params=_params(("arbitrary", "arbitrary")),
        name="cond_proj",
    )(c, w, b.reshape(2, 1, 3 * D))


def _group_layernorm(v, g, b):
    outs = []
    for k in range(SGU_GROUPS):
        sl = slice(k * GROUP_W, (k + 1) * GROUP_W)
        outs.append(_layernorm(v[:, sl], g[:, sl], b[:, sl]))
    return outs


def _mix0_in_kernel(x_ref, sh_ref, sc_ref, w_ref, lng_ref, lnb_ref, ws_ref, bst_ref,
                    xr_ref, ggr_ref, sgu_ref, *, tm):
    h = (x_ref[...] * (1.0 + sc_ref[...]) + sh_ref[...]).astype(BF16)
    xr_ref[...] = _dot(h, w_ref[:, 0:RNN_W])
    ggr_ref[...] = jax.nn.gelu(_dot(h, w_ref[:, RNN_W:2 * RNN_W]))
    gu = jax.nn.gelu(_dot(h, w_ref[:, 2 * RNN_W:2 * RNN_W + SGU_W]))
    gv = jax.nn.gelu(_dot(h, w_ref[:, 2 * RNN_W + SGU_W:]))
    vn = _group_layernorm(gv, lng_ref[...], lnb_ref[...])
    row = lax.broadcasted_iota(jnp.int32, (CHUNK, CHUNK), 0)
    col = lax.broadcasted_iota(jnp.int32, (CHUNK, CHUNK), 1)
    for k in range(SGU_GROUPS):
        ws = jnp.where(col <= row, ws_ref[k], 0.0).astype(BF16)
        vk = vn[k].astype(BF16)
        bias = bst_ref[:, k:k + 1]
        for c in range(tm // CHUNK):
            rows = slice(c * CHUNK, (c + 1) * CHUNK)
            cols = slice(k * GROUP_W, (k + 1) * GROUP_W)
            s = _dot(ws, vk[rows]) + bias
            sgu_ref[rows, cols] = gu[rows, cols] * s


def _mix0_in(x, mod, w_in, ln_g, ln_b, ws, bs_t, tm=512):
    t = x.shape[0]
    mod_args, mod_specs = _mod_specs(mod, 0, (0, 1), True, tm)
    out = jax.ShapeDtypeStruct((t, RNN_W), F32)
    return pl.pallas_call(
        functools.partial(_mix0_in_kernel, tm=tm),
        grid=(t // tm,),
        in_specs=[_row_spec(tm, D), *mod_specs, _resident((D, 4 * RNN_W)),
                  _resident((1, SGU_W)), _resident((1, SGU_W)),
                  _resident((SGU_GROUPS, CHUNK, CHUNK)), _resident((CHUNK, SGU_GROUPS))],
        out_specs=[_row_spec(tm, RNN_W)] * 3,
        out_shape=[out] * 3,
        compiler_params=_params(("arbitrary",)),
        name="mix0_in",
    )(x, *mod_args, w_in, ln_g, ln_b, ws, bs_t)


def _lru_gates(xc, wa_ref, ba, wx_ref, bx, lam):
    xb = xc.astype(BF16)
    lo, hi = xb[:, :RNN_HALF], xb[:, RNN_HALF:]
    r = jax.nn.sigmoid(jnp.concatenate([_dot(lo, wa_ref[0]), _dot(hi, wa_ref[1])], axis=1) + ba)
    i = jax.nn.sigmoid(jnp.concatenate([_dot(lo, wx_ref[0]), _dot(hi, wx_ref[1])], axis=1) + bx)
    log_a = (-LRU_C) * r * jax.nn.softplus(-lam)
    a = jnp.exp(log_a)
    u = jnp.sqrt(-jnp.tanh(log_a) * (a * a + 1.0)) * (i * xc)
    return a, u


def _mix0_rnn_kernel(xr_ref, ggr_ref, sgu_ref, x_ref, gt_ref, cw_ref, cb_ref, wa_ref, ba_ref,
                     wx_ref, bx_ref, lam_ref, wo_ref, g_ref, b_ref,
                     y_ref, ht_ref, buf_ref, a_ref, u_ref, hs_ref, hc_ref, *, ts):
    t = pl.program_id(1)

    @pl.when(t == 0)
    def _():
        buf_ref[0:8, :] = jnp.zeros((8, RNN_W), F32)
        hc_ref[...] = jnp.zeros((1, RNN_W), F32)

    xr = xr_ref[...]
    buf_ref[8:8 + ts, :] = xr
    cw = cw_ref[...]
    xc = (buf_ref[5:5 + ts, :] * cw[0:1] + buf_ref[6:6 + ts, :] * cw[1:2]
          + buf_ref[7:7 + ts, :] * cw[2:3] + xr * cw[3:4]) + cb_ref[...]
    buf_ref[0:8, :] = xr[ts - 8:ts, :]

    a, u = _lru_gates(xc, wa_ref, ba_ref[...], wx_ref, bx_ref[...], lam_ref[...])
    a_ref[...] = a
    u_ref[...] = u

    row = lax.broadcasted_iota(jnp.int32, (8, RNN_W), 0)

    def tile(k, h_prev):
        off = pl.multiple_of(k * 8, 8)
        aa = a_ref[pl.ds(off, 8), :]
        uu = u_ref[pl.ds(off, 8), :]
        for d in (1, 2, 4):
            keep = row >= d
            a_sh = jnp.where(keep, pltpu.roll(aa, d, 0), 1.0)
            u_sh = jnp.where(keep, pltpu.roll(uu, d, 0), 0.0)
            uu = uu + aa * u_sh
            aa = aa * a_sh
        h = aa * h_prev + uu
        hs_ref[pl.ds(off, 8), :] = h
        return h[7:8, :]

    h_last = lax.fori_loop(0, ts // 8, tile, hc_ref[...], unroll=4)
    hc_ref[...] = h_last

    @pl.when(t == pl.num_programs(1) - 1)
    def _():
        ht_ref[...] = h_last

    rnn = (hs_ref[...] * ggr_ref[...]).astype(BF16)
    out = _dot(rnn, wo_ref[0:RNN_W, :]) + _dot(sgu_ref[...].astype(BF16), wo_ref[RNN_W:, :])
    y = ALPHA * x_ref[...] + (1.0 + gt_ref[...]) * out
    y_ref[...] = _layernorm(y, g_ref[...], b_ref[...])


def _mix0_rnn(xr, ggr, sgu, x, mod, conv_w, conv_b, wa, ba, wx, bx, lam, w_out, ln_g, ln_b, ts=512):
    nt = SEQ // ts
    arr = mod.reshape(2, N_COND, 1, 3 * D)
    tile_spec = lambda w: pl.BlockSpec((ts, w), lambda b, t: (b * nt + t, 0))
    res = lambda shape: pl.BlockSpec(shape, lambda b, t: (0,) * len(shape), pipeline_mode=pl.Buffered(1))
    return pl.pallas_call(
        functools.partial(_mix0_rnn_kernel, ts=ts),
        grid=(N_PROMPT, nt),
        in_specs=[tile_spec(RNN_W), tile_spec(RNN_W), tile_spec(SGU_W), tile_spec(D),
                  pl.BlockSpec((None, None, 1, D), lambda b, t: (0, N_SAMPLE + b, 0, 2)),
                  res((CONV_W, RNN_W)), res((1, RNN_W)),
                  res((2, RNN_HALF, RNN_HALF)), res((1, RNN_W)),
                  res((2, RNN_HALF, RNN_HALF)), res((1, RNN_W)), res((1, RNN_W)),
                  res((D, D)), res((1, D)), res((1, D))],
        out_specs=[tile_spec(D), pl.BlockSpec((None, 1, RNN_W), lambda b, t: (b, 0, 0))],
        out_shape=[jax.ShapeDtypeStruct((N_PROMPT * SEQ, D), F32),
                   jax.ShapeDtypeStruct((N_PROMPT, 1, RNN_W), F32)],
        scratch_shapes=[pltpu.VMEM((ts + 8, RNN_W), F32), pltpu.VMEM((ts, RNN_W), F32),
                        pltpu.VMEM((ts, RNN_W), F32), pltpu.VMEM((ts, RNN_W), F32),
                        pltpu.VMEM((1, RNN_W), F32)],
        compiler_params=_params(("arbitrary", "arbitrary")),
        name="mix0_rnn",
    )(xr, ggr, sgu, x, arr, conv_w, conv_b, wa, ba, wx, bx, lam, w_out, ln_g, ln_b)


def _mix0_sample_kernel(x_ref, sh_ref, sc_ref, gt_ref, w_ref, st_ref, h0_ref, cw_ref, cb_ref,
                        wa_ref, ba_ref, wx_ref, bx_ref, lam_ref, lng_ref, lnb_ref, ws0_ref, bs0_ref,
                        wo_ref, g_ref, b_ref, y_ref, xr_ref, ht_ref, vn_ref):
    x = x_ref[...]
    h = (x * (1.0 + sc_ref[...]) + sh_ref[...]).astype(BF16)
    xr = _dot(h, w_ref[:, 0:RNN_W])
    ggr = jax.nn.gelu(_dot(h, w_ref[:, RNN_W:2 * RNN_W]))
    gu = jax.nn.gelu(_dot(h, w_ref[:, 2 * RNN_W:2 * RNN_W + SGU_W]))
    gv = jax.nn.gelu(_dot(h, w_ref[:, 2 * RNN_W + SGU_W:]))
    xr_ref[...] = xr
    cw = cw_ref[...]
    xc = (st_ref[0] * cw[0:1] + st_ref[1] * cw[1:2] + st_ref[2] * cw[2:3] + xr * cw[3:4]) + cb_ref[...]
    a, u = _lru_gates(xc, wa_ref, ba_ref[...], wx_ref, bx_ref[...], lam_ref[...])
    h_new = a * h0_ref[...] + u
    ht_ref[...] = h_new
    vn = jnp.concatenate(_group_layernorm(gv, lng_ref[...], lnb_ref[...]), axis=1)
    vn_ref[...] = vn
    sgu = gu * (ws0_ref[...] * vn + bs0_ref[...])
    rnn = (h_new * ggr).astype(BF16)
    out = _dot(rnn, wo_ref[0:RNN_W, :]) + _dot(sgu.astype(BF16), wo_ref[RNN_W:, :])
    y = ALPHA * x + (1.0 + gt_ref[...]) * out
    y_ref[...] = _layernorm(y, g_ref[...], b_ref[...])


def _mix0_sample(x, mod, w_in, state, h0, conv_w, conv_b, wa, ba, wx, bx, lam, lng, lnb, ws0, bs0,
                 w_out, ln_g, ln_b):
    mod_args, mod_specs = _mod_specs(mod, 0, (0, 1, 2), False, N_SAMPLE)
    full = lambda a: pl.BlockSpec(a.shape, lambda i: (0,) * a.ndim)
    rest = (w_in, state, h0, conv_w, conv_b, wa, ba, wx, bx, lam, lng, lnb, ws0, bs0, w_out, ln_g, ln_b)
    vec = jax.ShapeDtypeStruct((N_SAMPLE, RNN_W), F32)
    return pl.pallas_call(
        _mix0_sample_kernel,
        grid=(1,),
        in_specs=[full(x), *mod_specs, *[full(a) for a in rest]],
        out_specs=[_row_spec(N_SAMPLE, D)] + [_row_spec(N_SAMPLE, RNN_W)] * 3,
        out_shape=[jax.ShapeDtypeStruct((N_SAMPLE, D), F32), vec, vec, vec],
        compiler_params=_params(("arbitrary",)),
        name="mix0_sample",
    )(x, *mod_args, *rest)


def _ffn_kernel(x_ref, sh_ref, sc_ref, gt_ref, wg_ref, wu_ref, wd_ref, g_ref, b_ref, y_ref, acc_ref,
                *, bounds):
    x = x_ref[...]
    h = (x * (1.0 + sc_ref[...]) + sh_ref[...]).astype(BF16)
    for n, (lo, hi) in enumerate(bounds):
        act = (_silu(_dot(h, wg_ref[:, lo:hi])) * _dot(h, wu_ref[:, lo:hi])).astype(BF16)
        part = _dot(act, wd_ref[lo:hi, :])
        if n == 0:
            acc_ref[...] = part
        else:
            acc_ref[...] += part
    y = ALPHA * x + (1.0 + gt_ref[...]) * acc_ref[...]
    y_ref[...] = _layernorm(y, g_ref[...], b_ref[...])


def _ffn(x, mod, prompt, wg, wu, wd, ln_g, ln_b, tm):
    t = x.shape[0]
    step = 512
    bounds = tuple((lo, min(lo + step, D_FF)) for lo in range(0, D_FF, step))
    mod_args, mod_specs = _mod_specs(mod, 0, (0, 1, 2), prompt, tm)
    return pl.pallas_call(
        functools.partial(_ffn_kernel, bounds=bounds),
        grid=(t // tm,),
        in_specs=[_row_spec(tm, D), *mod_specs, _resident((D, D_FF)), _resident((D, D_FF)),
                  _resident((D_FF, D)), _resident((1, D)), _resident((1, D))],
        out_specs=_row_spec(tm, D),
        out_shape=jax.ShapeDtypeStruct((t, D), F32),
        scratch_shapes=[pltpu.VMEM((tm, D), F32)],
        compiler_params=_params(("arbitrary",)),
        name="ffn_prompt" if prompt else "ffn_sample",
    )(x, *mod_args, wg, wu, wd, ln_g, ln_b)


def _rope(y, cos, sin_signed):
    lane = lax.broadcasted_iota(jnp.int32, y.shape, 1)
    first_half = (lane % MAP_W) < (ROT // 2)
    partner = jnp.where(first_half, pltpu.roll(y, HEAD_W - ROT // 2, 1), pltpu.roll(y, ROT // 2, 1))
    return y * cos + partner * sin_signed


def _qkv_kernel(x_ref, sh_ref, sc_ref, w_ref, cos_ref, sin_ref, q_ref, k_ref, v_ref):
    h = (x_ref[...] * (1.0 + sc_ref[...]) + sh_ref[...]).astype(BF16)
    cos, sin = cos_ref[...], sin_ref[...]
    width = 4 * HEAD_W
    for out_ref, base, rotate in ((q_ref, 0, True), (k_ref, D, True), (v_ref, 2 * D, False)):
        for j in range(D // width):
            y = _dot(h, w_ref[:, base + j * width:base + (j + 1) * width])
            if rotate:
                y = jnp.concatenate([_rope(y[:, n * HEAD_W:(n + 1) * HEAD_W], cos, sin)
                                     for n in range(width // HEAD_W)], axis=1)
            out_ref[:, j * width:(j + 1) * width] = y


def _qkv(x, mod, prompt, w_qkv, cos, sin, tm):
    t = x.shape[0]
    mod_args, mod_specs = _mod_specs(mod, 1, (0, 1), prompt, tm)
    if prompt:
        tab_spec = pl.BlockSpec((tm, HEAD_W), lambda i: (i % (SEQ // tm), 0))
    else:
        tab_spec = pl.BlockSpec((1, HEAD_W), lambda i: (0, 0))
    out = jax.ShapeDtypeStruct((t, D), F32)
    return pl.pallas_call(
        _qkv_kernel,
        grid=(t // tm,),
        in_specs=[_row_spec(tm, D), *mod_specs, _resident((D, 3 * D)), tab_spec, tab_spec],
        out_specs=[_row_spec(tm, D)] * 3,
        out_shape=[out] * 3,
        compiler_params=_params(("arbitrary",)),
        name="qkv_prompt" if prompt else "qkv_sample",
    )(x, *mod_args, w_qkv, cos, sin)


def _rope_tables(pos):
    half = ROT // 2
    inv = ROPE_THETA ** (-(jnp.arange(half, dtype=F32) * 2.0 / ROT))
    ang = pos.astype(F32)[:, None] * inv
    ones = jnp.ones((pos.shape[0], MAP_W - ROT), F32)
    cos = jnp.concatenate([jnp.cos(ang), jnp.cos(ang), ones], axis=1)
    sin = jnp.concatenate([-jnp.sin(ang), jnp.sin(ang), 0.0 * ones], axis=1)
    return jnp.tile(cos, (1, 2)), jnp.tile(sin, (1, 2))


def _diff_lambda(lp):
    s1 = jnp.sum(lp[0:1] * lp[1:2], axis=-1, keepdims=True)
    s2 = jnp.sum(lp[2:3] * lp[3:4], axis=-1, keepdims=True)
    return jnp.exp(s1) - jnp.exp(s2) + LAM_INIT


def _flash_step(lp_ref, q_ref, k_ref, v_ref, o_ref, kb_ref, vb_ref, qs_ref, m_ref, acc_ref, qi, tq):
    @pl.when(qi == 0)
    def _():
        kb_ref[...] = k_ref[...].astype(BF16)
        vb_ref[:, 0:HEAD_W] = v_ref[...].astype(BF16)
        vb_ref[:, HEAD_W:] = jnp.ones((SEQ, HEAD_W), BF16)

    q = q_ref[...] * ATT_SCALE
    lane = lax.broadcasted_iota(jnp.int32, q.shape, 1)
    qs_ref[0] = jnp.where(lane < MAP_W, q, 0.0).astype(BF16)
    qs_ref[1] = jnp.where(lane >= MAP_W, q, 0.0).astype(BF16)
    m_ref[...] = jnp.full(m_ref.shape, NEG_INF, F32)
    acc_ref[...] = jnp.zeros(acc_ref.shape, F32)

    def block(j, diagonal):
        off = pl.multiple_of(j * tq, tq)
        kj = kb_ref[pl.ds(off, tq), :]
        vj = vb_ref[pl.ds(off, tq), :]
        if diagonal:
            row = lax.broadcasted_iota(jnp.int32, (tq, tq), 0)
            col = lax.broadcasted_iota(jnp.int32, (tq, tq), 1)
            visible = col <= row
        for m in range(2):
            s = _dot_nt(qs_ref[m], kj)
            if diagonal:
                s = jnp.where(visible, s, NEG_INF)
            m_prev = m_ref[m]
            m_new = jnp.maximum(m_prev, jnp.max(s, axis=-1, keepdims=True))
            p = jnp.exp(s - m_new).astype(BF16)
            acc_ref[m] = jnp.exp(m_prev - m_new) * acc_ref[m] + _dot(p, vj)
            m_ref[m] = m_new

    def below_diagonal(j, carry):
        block(j, False)
        return carry

    lax.fori_loop(0, qi, below_diagonal, 0)
    block(qi, True)
    o1 = acc_ref[0][:, 0:HEAD_W] / acc_ref[0][:, HEAD_W:]
    o2 = acc_ref[1][:, 0:HEAD_W] / acc_ref[1][:, HEAD_W:]
    o_ref[...] = o1 - _diff_lambda(lp_ref[...]) * o2


def _decode_step(lp_ref, q_ref, kn_ref, vn_ref, kc_refs, vc_refs, o_ref, qb_ref, m_ref, l_ref, acc_ref, first, last):
    @pl.when(first)
    def _():
        q = q_ref[...] * ATT_SCALE
        lane = lax.broadcasted_iota(jnp.int32, q.shape, 1)
        qb_ref[0:N_HEADS, :] = jnp.where(lane < MAP_W, q, 0.0)
        qb_ref[N_HEADS:, :] = jnp.where(lane >= MAP_W, q, 0.0)
        m_ref[...] = jnp.full(m_ref.shape, NEG_INF, F32)
        l_ref[...] = jnp.zeros(l_ref.shape, F32)
        acc_ref[...] = jnp.zeros(acc_ref.shape, F32)

    rows = PAGE * N_HEADS
    qb = qb_ref[...].astype(BF16)
    r = lax.broadcasted_iota(jnp.int32, (2 * N_HEADS, rows), 0)
    c = lax.broadcasted_iota(jnp.int32, (2 * N_HEADS, rows), 1)
    own_head = c % N_HEADS == r % N_HEADS
    scores = []
    for kc_ref in kc_refs:
        s = _dot_nt(qb, kc_ref[...].reshape(rows, HEAD_W).astype(BF16))
        scores.append(jnp.where(own_head, s, NEG_INF))
    m_prev = m_ref[...]
    m_new = m_prev
    for s in scores:
        m_new = jnp.maximum(m_new, jnp.max(s, axis=-1, keepdims=True))
    l_new = jnp.exp(m_prev - m_new) * l_ref[...]
    acc = jnp.exp(m_prev - m_new) * acc_ref[...]
    for s, vc_ref in zip(scores, vc_refs):
        p = jnp.exp(s - m_new)
        l_new = l_new + jnp.sum(p, axis=-1, keepdims=True)
        acc = acc + _dot(p.astype(BF16), vc_ref[...].reshape(rows, HEAD_W).astype(BF16))
    l_ref[...] = l_new
    acc_ref[...] = acc
    m_ref[...] = m_new

    @pl.when(last)
    def _():
        k_new = jnp.concatenate([kn_ref[...], kn_ref[...]], axis=0)
        v_new = jnp.concatenate([vn_ref[...], vn_ref[...]], axis=0)
        s_new = jnp.sum(qb_ref[...] * k_new, axis=-1, keepdims=True)
        m_old = m_ref[...]
        m_fin = jnp.maximum(m_old, s_new)
        a_old = jnp.exp(m_old - m_fin)
        p_new = jnp.exp(s_new - m_fin)
        l_fin = a_old * l_ref[...] + p_new
        o_maps = (a_old * acc_ref[...] + p_new * v_new) / l_fin
        o_ref[...] = o_maps[0:N_HEADS] - _diff_lambda(lp_ref[...]) * o_maps[N_HEADS:]


def _attn_kernel(pt_ref, lp_ref, q_ref, k_ref, v_ref, qs_ref, kn_ref, vn_ref, kc_ref, vc_ref, o_ref, os_ref,
                 kb_ref, vb_ref, qm_ref, m_ref, acc_ref, kpg_ref, vpg_ref, sem, qb_ref, ms_ref, ls_ref, accs_ref,
                 *, tq, n_pg, runs):
    nq = pl.num_programs(2)
    qi = pl.program_id(2)
    s = (pl.program_id(0) * pl.num_programs(1) + pl.program_id(1)) * nq + qi
    n_steps = pl.num_programs(0) * pl.num_programs(1) * nq
    slot = s % 2

    def page_copies(step, into):
        token, run = step // runs, step % runs
        for g in range(n_pg):
            page = pt_ref[token, run * n_pg + g]
            yield pltpu.make_async_copy(kc_ref.at[page], kpg_ref.at[into, g], sem.at[into, 0])
            yield pltpu.make_async_copy(vc_ref.at[page], vpg_ref.at[into, g], sem.at[into, 1])

    @pl.when(s == 0)
    def _():
        for copy in page_copies(s, slot):
            copy.start()

    @pl.when(s + 1 < n_steps)
    def _():
        for copy in page_copies(s + 1, 1 - slot):
            copy.start()

    _flash_step(lp_ref, q_ref, k_ref, v_ref, o_ref, kb_ref, vb_ref, qm_ref, m_ref, acc_ref, qi, tq)

    for copy in page_copies(s, slot):
        copy.wait()
    _decode_step(lp_ref, qs_ref, kn_ref, vn_ref,
                 [kpg_ref.at[slot, g] for g in range(n_pg)], [vpg_ref.at[slot, g] for g in range(n_pg)],
                 os_ref, qb_ref, ms_ref, ls_ref, accs_ref, s % runs == 0, s % runs == runs - 1)


def _attention(lp, q, k, v, q_s, k_s, v_s, cache_k, cache_v, page_table, tq=512):
    nq = SEQ // tq
    n_steps = N_PROMPT * N_HEADS * nq
    n_pages = page_table.shape[1]
    runs = n_steps // N_SAMPLE
    n_pg = n_pages // runs
    assert runs * N_SAMPLE == n_steps and n_pg * runs == n_pages
    q3, k3, v3 = (a.reshape(N_PROMPT, SEQ, D) for a in (q, k, v))
    heads = lambda a: a.reshape(N_SAMPLE, N_HEADS, HEAD_W)
    kc = cache_k.reshape(-1, PAGE, N_HEADS, HEAD_W)
    vc = cache_v.reshape(-1, PAGE, N_HEADS, HEAD_W)
    q_spec = pl.BlockSpec((None, tq, HEAD_W), lambda b, h, i, pt: (b, i, h))
    kv_spec = pl.BlockSpec((None, SEQ, HEAD_W), lambda b, h, i, pt: (b, 0, h))
    token_spec = pl.BlockSpec((None, N_HEADS, HEAD_W),
                              lambda b, h, i, pt: (((b * N_HEADS + h) * nq + i) // runs, 0, 0))
    any_spec = pl.BlockSpec(memory_space=pl.ANY)
    n_maps = 2 * N_HEADS
    page_buf = pltpu.VMEM((2, n_pg, PAGE, N_HEADS, HEAD_W), F32)
    o_p, o_s = pl.pallas_call(
        functools.partial(_attn_kernel, tq=tq, n_pg=n_pg, runs=runs),
        grid_spec=pltpu.PrefetchScalarGridSpec(
            num_scalar_prefetch=1,
            grid=(N_PROMPT, N_HEADS, nq),
            in_specs=[pl.BlockSpec((4, MAP_W), lambda b, h, i, pt: (0, 0)), q_spec, kv_spec, kv_spec,
                      token_spec, token_spec, token_spec, any_spec, any_spec],
            out_specs=[q_spec, token_spec],
            scratch_shapes=[pltpu.VMEM((SEQ, HEAD_W), BF16), pltpu.VMEM((SEQ, 2 * HEAD_W), BF16),
                            pltpu.VMEM((2, tq, HEAD_W), BF16), pltpu.VMEM((2, tq, 1), F32),
                            pltpu.VMEM((2, tq, 2 * HEAD_W), F32),
                            page_buf, page_buf, pltpu.SemaphoreType.DMA((2, 2)),
                            pltpu.VMEM((n_maps, HEAD_W), F32), pltpu.VMEM((n_maps, 1), F32),
                            pltpu.VMEM((n_maps, 1), F32), pltpu.VMEM((n_maps, HEAD_W), F32)]),
        out_shape=[jax.ShapeDtypeStruct((N_PROMPT, SEQ, D), F32),
                   jax.ShapeDtypeStruct((N_SAMPLE, N_HEADS, HEAD_W), F32)],
        compiler_params=_params(("arbitrary",) * 3),
        name="diff_attn",
    )(page_table, lp, q3, k3, v3, heads(q_s), heads(k_s), heads(v_s), kc, vc)
    return o_p.reshape(N_PROMPT * SEQ, D), o_s.reshape(N_SAMPLE, D)


def _attn_out_kernel(o_ref, x_ref, gt_ref, sg_ref, wo_ref, g_ref, b_ref, y_ref):
    o = o_ref[...]
    heads = []
    for n in range(N_HEADS):
        oh = o[:, n * HEAD_W:(n + 1) * HEAD_W]
        ms = jnp.mean(oh * oh, axis=-1, keepdims=True)
        heads.append((oh * lax.rsqrt(ms + LN_EPS) * sg_ref[...]) * (1.0 - LAM_INIT))
    out = _dot(jnp.concatenate(heads, axis=1).astype(BF16), wo_ref[...])
    y = ALPHA * x_ref[...] + (1.0 + gt_ref[...]) * out
    y_ref[...] = _layernorm(y, g_ref[...], b_ref[...])


def _attn_out(o, x, mod, prompt, subln_g, w_o, ln_g, ln_b, tm):
    t = x.shape[0]
    mod_args, mod_specs = _mod_specs(mod, 1, (2,), prompt, tm)
    return pl.pallas_call(
        _attn_out_kernel,
        grid=(t // tm,),
        in_specs=[_row_spec(tm, D), _row_spec(tm, D), *mod_specs, _resident((1, HEAD_W)),
                  _resident((D, D)), _resident((1, D)), _resident((1, D))],
        out_specs=_row_spec(tm, D),
        out_shape=jax.ShapeDtypeStruct((t, D), F32),
        compiler_params=_params(("arbitrary",)),
        name="attn_out_prompt" if prompt else "attn_out_sample",
    )(o, x, *mod_args, subln_g, w_o, ln_g, ln_b)


ROUTE_IDX = N_EXPERTS
ROUTE_GATE = N_EXPERTS + 2
MOE_TM = 1024
MOE_HALF = MOE_TM // 2
MOE_TF = 512
ROW_TILE = 256


def _route(logits):
    lane = lax.broadcasted_iota(jnp.int32, logits.shape, 1)
    z = jnp.where(lane < N_EXPERTS, logits, NEG_INF)
    m1 = jnp.max(z, axis=-1, keepdims=True)
    i1 = jnp.min(jnp.where(z == m1, lane, LANES), axis=-1, keepdims=True)
    z2 = jnp.where(lane == i1, NEG_INF, z)
    m2 = jnp.max(z2, axis=-1, keepdims=True)
    i2 = jnp.min(jnp.where(z2 == m2, lane, LANES), axis=-1, keepdims=True)
    e2 = jnp.exp(m2 - m1)
    g1 = 1.0 / (1.0 + e2)
    g2 = e2 / (1.0 + e2)
    out = jnp.where(lane == ROUTE_IDX, i1.astype(F32), 0.0) + jnp.where(lane == ROUTE_IDX + 1, i2.astype(F32), 0.0)
    return out + jnp.where(lane == ROUTE_GATE, g1, 0.0) + jnp.where(lane == ROUTE_GATE + 1, g2, 0.0)


def _router_kernel(x_ref, sh_ref, sc_ref, wr_ref, r_ref):
    h = (x_ref[...] * (1.0 + sc_ref[...]) + sh_ref[...]).astype(BF16)
    r_ref[...] = _route(_dot(h, wr_ref[...].astype(BF16)))


def _router(x, mod, prompt, w_router, tm):
    t = x.shape[0]
    arrs, specs = _mod_specs(mod, 1, (0, 1), prompt, tm)
    return pl.pallas_call(
        _router_kernel,
        grid=(t // tm,),
        in_specs=[_row_spec(tm, D), *specs, _resident((D, LANES))],
        out_specs=_row_spec(tm, LANES),
        out_shape=jax.ShapeDtypeStruct((t, LANES), F32),
        compiler_params=_params(("arbitrary",)),
        name="router_prompt" if prompt else "router_sample",
    )(x, *arrs, w_router)


def _moe_plan(expert_ids, tm, n_tiles):
    onehot = (expert_ids[:, None] == jnp.arange(N_EXPERTS, dtype=jnp.int32)[None, :]).astype(jnp.int32)
    rank = jnp.cumsum(onehot, axis=0) - onehot
    counts = jnp.sum(onehot, axis=0)
    tiles_per = (counts + tm - 1) // tm
    tile_end = jnp.cumsum(tiles_per)
    tile_start = tile_end - tiles_per
    pos = jnp.sum(onehot * (tile_start * tm + rank), axis=1)
    n_used = tile_end[-1]
    tile = jnp.minimum(jnp.arange(n_tiles, dtype=jnp.int32), n_used - 1)
    expert = jnp.sum((tile[:, None] >= tile_end[None, :]).astype(jnp.int32), axis=1)
    rows = jnp.clip(counts[expert] - (tile - tile_start[expert]) * tm, 0, tm)
    return pos.astype(jnp.int32), tile, expert.astype(jnp.int32), rows.astype(jnp.int32), n_used.reshape(1)


def _token_rows(x):
    return x.reshape(x.shape[0], D // LANES, LANES)


def _wait_tile_of_rows(src_ref, dst_ref, sem, choices=2):
    for _ in range(choices):
        pltpu.make_async_copy(src_ref, dst_ref, sem).wait()


def _dispatch_kernel(pos_ref, x_ref, sh_ref, sc_ref, xg_in_ref, xg_ref, h_ref, sem, *, tr, n_tok):
    del xg_in_ref
    i = pl.program_id(0)
    n = pl.num_programs(0)
    slot = i % 2
    sized = xg_ref.at[pl.ds(0, tr)]

    @pl.when(i >= 2)
    def _():
        _wait_tile_of_rows(h_ref.at[slot], sized, sem.at[slot])

    h_ref[slot] = _token_rows(x_ref[...] * (1.0 + sc_ref[...]) + sh_ref[...])

    def send(r, carry):
        for k in range(2):
            dst = pos_ref[k * n_tok + i * tr + r]
            pltpu.make_async_copy(h_ref.at[slot, r], xg_ref.at[dst], sem.at[slot]).start()
        return carry

    lax.fori_loop(0, tr, send, 0, unroll=8)

    @pl.when(i == n - 1)
    def _():
        @pl.when(i >= 1)
        def _():
            _wait_tile_of_rows(h_ref.at[1 - slot], sized, sem.at[1 - slot])
        _wait_tile_of_rows(h_ref.at[slot], sized, sem.at[slot])


def _dispatch(x, mod, prompt, pos, xg, tr):
    t = x.shape[0]
    arrs, specs = _mod_specs(mod, 1, (0, 1), prompt, tr)
    any_spec = pl.BlockSpec(memory_space=pl.ANY)
    return pl.pallas_call(
        functools.partial(_dispatch_kernel, tr=tr, n_tok=t),
        grid_spec=pltpu.PrefetchScalarGridSpec(
            num_scalar_prefetch=1,
            grid=(t // tr,),
            in_specs=[pl.BlockSpec((tr, D), lambda i, pos: (i, 0)), *specs, any_spec],
            out_specs=any_spec,
            scratch_shapes=[pltpu.VMEM((2, tr, D // LANES, LANES), F32), pltpu.SemaphoreType.DMA((2,))]),
        out_shape=jax.ShapeDtypeStruct(xg.shape, F32),
        input_output_aliases={4: 0},
        compiler_params=_params(("arbitrary",)),
        name="moe_dispatch_prompt" if prompt else "moe_dispatch_sample",
    )(pos, x, *arrs, xg)


def _expert_kernel(tile_ref, exp_ref, rows_ref, used_ref, xg_ref, wg_ref, wu_ref, wd_ref, yg_ref,
                   hb_ref, acc_ref, wgb_ref, wub_ref, wdb_ref):
    i = pl.program_id(0)
    f = pl.program_id(1)
    last_f = pl.num_programs(1) - 1
    live = i < used_ref[0]

    @pl.when(live)
    def _():
        wgb_ref[...] = wg_ref[...].astype(BF16)
        wub_ref[...] = wu_ref[...].astype(BF16)
        wdb_ref[...] = wd_ref[...].astype(BF16)

    for half in range(MOE_TM // MOE_HALF):
        rows = slice(half * MOE_HALF, (half + 1) * MOE_HALF)
        occupied = live & (rows_ref[i] > half * MOE_HALF)

        @pl.when(occupied)
        def _():
            @pl.when(f == 0)
            def _():
                hb_ref[rows, :] = xg_ref[rows].reshape(MOE_HALF, D).astype(BF16)

            h = hb_ref[rows, :]
            act = (_silu(_dot(h, wgb_ref[...])) * _dot(h, wub_ref[...])).astype(BF16)
            part = _dot(act, wdb_ref[...])

            @pl.when(f == 0)
            def _():
                acc_ref[rows, :] = part

            @pl.when(f > 0)
            def _():
                acc_ref[rows, :] += part

            @pl.when(f == last_f)
            def _():
                yg_ref[rows] = _token_rows(acc_ref[rows, :])

        @pl.when(jnp.logical_not(occupied) & (f == 0))
        def _():
            yg_ref[rows] = jnp.zeros((MOE_HALF, D // LANES, LANES), F32)


def _experts(xg, tile, expert, rows, n_used, wg, wu, wd):
    n_tiles = xg.shape[0] // MOE_TM
    nf = D_FF_E // MOE_TF

    def f_block(i, f, used):
        return jnp.where(i < used[0], f, nf - 1)

    return pl.pallas_call(
        _expert_kernel,
        grid_spec=pltpu.PrefetchScalarGridSpec(
            num_scalar_prefetch=4,
            grid=(n_tiles, nf),
            in_specs=[pl.BlockSpec((MOE_TM, D // LANES, LANES), lambda i, f, tile, exp, rows, used: (tile[i], 0, 0)),
                      pl.BlockSpec((None, D, MOE_TF),
                                   lambda i, f, tile, exp, rows, used: (exp[i], 0, f_block(i, f, used))),
                      pl.BlockSpec((None, D, MOE_TF),
                                   lambda i, f, tile, exp, rows, used: (exp[i], 0, f_block(i, f, used))),
                      pl.BlockSpec((None, MOE_TF, D),
                                   lambda i, f, tile, exp, rows, used: (exp[i], f_block(i, f, used), 0))],
            out_specs=pl.BlockSpec((MOE_TM, D // LANES, LANES), lambda i, f, tile, exp, rows, used: (i, 0, 0)),
            scratch_shapes=[pltpu.VMEM((MOE_TM, D), BF16), pltpu.VMEM((MOE_TM, D), F32),
                            pltpu.VMEM((D, MOE_TF), BF16), pltpu.VMEM((D, MOE_TF), BF16),
                            pltpu.VMEM((MOE_TF, D), BF16)]),
        out_shape=jax.ShapeDtypeStruct(xg.shape, F32),
        compiler_params=_params(("arbitrary", "arbitrary")),
        name="moe_experts",
    )(tile, expert, rows, n_used, xg, wg, wu, wd)


def _combine_kernel(pos_ref, yg_ref, r_ref, x_ref, gt_ref, g_ref, b_ref, y_ref, buf_ref, sem, *, tr, n_tok):
    i = pl.program_id(0)
    n = pl.num_programs(0)
    slot = i % 2

    def fetch(step, into):
        def one(r, carry):
            for k in range(2):
                src = pos_ref[k * n_tok + step * tr + r]
                pltpu.make_async_copy(yg_ref.at[src], buf_ref.at[into, k, r], sem.at[into]).start()
            return carry
        lax.fori_loop(0, tr, one, 0, unroll=8)

    @pl.when(i == 0)
    def _():
        fetch(0, 0)

    @pl.when(i + 1 < n)
    def _():
        fetch(i + 1, 1 - slot)

    _wait_tile_of_rows(yg_ref.at[pl.ds(0, tr)], buf_ref.at[slot, 0], sem.at[slot])
    route = r_ref[...]
    first = buf_ref[slot, 0].reshape(tr, D)
    second = buf_ref[slot, 1].reshape(tr, D)
    moe = route[:, ROUTE_GATE:ROUTE_GATE + 1] * first + route[:, ROUTE_GATE + 1:ROUTE_GATE + 2] * second
    y = ALPHA * x_ref[...] + (1.0 + gt_ref[...]) * moe
    y_ref[...] = _layernorm(y, g_ref[...], b_ref[...])


def _combine(yg, pos, route, x, mod, prompt, ln_g, ln_b, tr):
    t = x.shape[0]
    arrs, specs = _mod_specs(mod, 1, (2,), prompt, tr)
    return pl.pallas_call(
        functools.partial(_combine_kernel, tr=tr, n_tok=t),
        grid_spec=pltpu.PrefetchScalarGridSpec(
            num_scalar_prefetch=1,
            grid=(t // tr,),
            in_specs=[pl.BlockSpec(memory_space=pl.ANY),
                      pl.BlockSpec((tr, LANES), lambda i, pos: (i, 0)),
                      pl.BlockSpec((tr, D), lambda i, pos: (i, 0)), *specs,
                      pl.BlockSpec((1, D), lambda i, pos: (0, 0)), pl.BlockSpec((1, D), lambda i, pos: (0, 0))],
            out_specs=pl.BlockSpec((tr, D), lambda i, pos: (i, 0)),
            scratch_shapes=[pltpu.VMEM((2, 2, tr, D // LANES, LANES), F32), pltpu.SemaphoreType.DMA((2,))]),
        out_shape=jax.ShapeDtypeStruct((t, D), F32),
        compiler_params=_params(("arbitrary",)),
        name="moe_combine_prompt" if prompt else "moe_combine_sample",
    )(pos, yg, route, x, *arrs, ln_g, ln_b)


def _moe(xp, xs, mod, w_router, wg, wu, wd, ln_g, ln_b):
    tp, ts = xp.shape[0], xs.shape[0]
    w_r = jnp.pad(w_router, ((0, 0), (0, LANES - N_EXPERTS)))
    route_p = _router(xp, mod, True, w_r, tm=512)
    route_s = _router(xs, mod, False, w_r, tm=ts)
    ids = lambda r: r[:, ROUTE_IDX:ROUTE_IDX + 2].astype(jnp.int32).T.reshape(-1)
    n_pairs = 2 * (tp + ts)
    n_tiles = (n_pairs + N_EXPERTS * (MOE_TM - 1)) // MOE_TM
    pos, tile, expert, rows, n_used = _moe_plan(jnp.concatenate([ids(route_p), ids(route_s)]), MOE_TM, n_tiles)
    pos_p, pos_s = pos[:2 * tp], pos[2 * tp:]
    xg = jnp.zeros((n_tiles * MOE_TM, D // LANES, LANES), F32)
    xg = _dispatch(xp, mod, True, pos_p, xg, ROW_TILE)
    xg = _dispatch(xs, mod, False, pos_s, xg, ts)
    yg = _experts(xg, tile, expert, rows, n_used, wg, wu, wd)
    yp = _combine(yg, pos_p, route_p, xp, mod, True, ln_g, ln_b, ROW_TILE)
    ys = _combine(yg, pos_s, route_s, xs, mod, False, ln_g, ln_b, ts)
    return yp, ys


def _block_diag(w):
    per = RNN_HALF // w.shape[-1]
    halves = [jax.scipy.linalg.block_diag(*[w[n * per + j] for j in range(per)]) for n in range(2)]
    return jnp.stack(halves).astype(BF16)


def kernel(x_prompt, x_sample, state_rglru_conv, state_rglru_h, cache_k, cache_v, page_table, c_prompt, c_sample, mix_mod_w, mix_mod_b, mix_ln_g, mix_ln_b, ffn_mod_w, ffn_mod_b, ffn_ln_g, ffn_ln_b, w_in0, conv_w, conv_b, lru_wa, lru_ba, lru_wx, lru_bx, lru_lambda, sgu_ln_g, sgu_ln_b, sgu_ws, sgu_bs, w_out0, ffn_w_gate, ffn_w_up, ffn_w_down, w_qkv, lambda_q1, lambda_k1, lambda_q2, lambda_k2, subln_g, w_o, w_router, moe_w_gate, moe_w_up, moe_w_down):
    past = page_table.shape[1] * PAGE
    xp = x_prompt.reshape(N_PROMPT * SEQ, D)
    xs = x_sample.reshape(N_SAMPLE, D)
    vec = lambda a: a.reshape(1, -1)

    cond = jnp.concatenate([c_sample, c_prompt], axis=0)
    mix_mod = _cond_proj(cond, mix_mod_w, mix_mod_b)
    ffn_mod = _cond_proj(cond, ffn_mod_w, ffn_mod_b)

    w_in = w_in0[0].astype(BF16)
    w_out = w_out0[0].astype(BF16)
    wa, wx = _block_diag(lru_wa[0]), _block_diag(lru_wx[0])
    lru = (conv_w[0], vec(conv_b[0]), wa, vec(lru_ba[0]), wx, vec(lru_bx[0]), vec(lru_lambda[0]))
    sgu_g, sgu_b = vec(sgu_ln_g[0]), vec(sgu_ln_b[0])
    ln0 = (vec(mix_ln_g[0]), vec(mix_ln_b[0]))

    xr_p, ggr_p, sgu_p = _mix0_in(xp, mix_mod, w_in, sgu_g, sgu_b, sgu_ws[0], sgu_bs[0].T)
    xp, ht_p = _mix0_rnn(xr_p, ggr_p, sgu_p, xp, mix_mod, *lru, w_out, *ln0)

    ws0 = vec(jnp.repeat(sgu_ws[0][:, 0, 0], GROUP_W))
    bs0 = vec(jnp.repeat(sgu_bs[0][:, 0], GROUP_W))
    state = jnp.swapaxes(state_rglru_conv[0], 0, 1)
    xs, xr_s, ht_s, vn_s = _mix0_sample(xs, mix_mod, w_in, state, state_rglru_h[0], *lru,
                                        sgu_g, sgu_b, ws0, bs0, w_out, *ln0)

    ffn_w = (ffn_w_gate[0].astype(BF16), ffn_w_up[0].astype(BF16), ffn_w_down[0].astype(BF16),
             vec(ffn_ln_g[0]), vec(ffn_ln_b[0]))
    xp = _ffn(xp, ffn_mod, True, *ffn_w, tm=512)
    xs = _ffn(xs, ffn_mod, False, *ffn_w, tm=N_SAMPLE)

    wqkv = w_qkv[0].astype(BF16)
    lp = jnp.stack([lambda_q1[0], lambda_k1[0], lambda_q2[0], lambda_k2[0]])
    cos_p, sin_p = _rope_tables(jnp.arange(SEQ))
    cos_s, sin_s = _rope_tables(jnp.full((1,), past))
    q_p, k_p, v_p = _qkv(xp, mix_mod, True, wqkv, cos_p, sin_p, tm=512)
    q_s, k_s, v_s = _qkv(xs, mix_mod, False, wqkv, cos_s, sin_s, tm=N_SAMPLE)
    o_p, o_s = _attention(lp, q_p, k_p, v_p, q_s, k_s, v_s, cache_k, cache_v, page_table)
    att_w = (vec(subln_g[0]), w_o[0].astype(BF16), vec(mix_ln_g[1]), vec(mix_ln_b[1]))
    xp = _attn_out(o_p, xp, mix_mod, True, *att_w, tm=512)
    xs = _attn_out(o_s, xs, mix_mod, False, *att_w, tm=N_SAMPLE)

    xp, xs = _moe(xp, xs, ffn_mod, w_router[0], moe_w_gate[0], moe_w_up[0], moe_w_down[0],
                  vec(ffn_ln_g[1]), vec(ffn_ln_b[1]))

    new_conv_s = jnp.concatenate([state_rglru_conv[0][:, 1:], xr_s[:, None]], axis=1)
    return (xp.reshape(N_PROMPT, SEQ, D), xs.reshape(N_SAMPLE, 1, D),
            xr_p.reshape(N_PROMPT, SEQ, RNN_W)[:, SEQ - (CONV_W - 1):][None],
            ht_p.reshape(1, N_PROMPT, RNN_W),
            new_conv_s[None], ht_s[None], vn_s.reshape(1, N_SAMPLE, 1, SGU_W),
            k_p.reshape(1, N_PROMPT, SEQ, N_HEADS, HEAD_W), v_p.reshape(1, N_PROMPT, SEQ, N_HEADS, HEAD_W),
            k_s.reshape(1, N_SAMPLE, 1, N_HEADS, HEAD_W), v_s.reshape(1, N_SAMPLE, 1, N_HEADS, HEAD_W))
```

```python
import functools
import math

import jax
import jax.numpy as jnp
from jax import lax
from jax.experimental import pallas as pl
from jax.experimental.pallas import tpu as pltpu

F32 = jnp.float32
BF16 = jnp.bfloat16

D = 1024
SEQ = 2048
N_PROMPT = 8
N_SAMPLE = 32
N_COND = N_PROMPT + N_SAMPLE
PAGE = 128
RNN_W = 512
RNN_HALF = 256
SGU_W = 512
SGU_GROUPS = 4
GROUP_W = 128
CHUNK = 128
CONV_W = 4
LRU_C = 8.0
N_HEADS = 8
HEAD_W = 128
MAP_W = 64
ROT = 16
ROPE_THETA = 500000.0
D_FF = 2816
N_EXPERTS = 8
D_FF_E = 3584
LN_EPS = 1e-5
ALPHA = (2 * 2) ** 0.25
LAM_INIT = 0.8 - 0.6 * math.exp(-0.3 * 1)
ATT_SCALE = MAP_W ** -0.5
LANES = 128
NEG_INF = float("-inf")

VMEM_LIMIT = 56 * 1024 * 1024


def _params(sem):
    return pltpu.CompilerParams(dimension_semantics=sem, vmem_limit_bytes=VMEM_LIMIT)


def _dot(a, b):
    return jnp.dot(a, b, preferred_element_type=F32)


def _dot_nt(a, b):
    return lax.dot_general(a, b, (((1,), (1,)), ((), ())), preferred_element_type=F32)


def _layernorm(y, g, b):
    mu = jnp.mean(y, axis=-1, keepdims=True)
    yc = y - mu
    var = jnp.mean(yc * yc, axis=-1, keepdims=True)
    return yc * lax.rsqrt(var + LN_EPS) * g + b


def _silu(x):
    return x * jax.nn.sigmoid(x)


def _resident(shape):
    n = len(shape)
    return pl.BlockSpec(shape, lambda *_: (0,) * n, pipeline_mode=pl.Buffered(1))


def _row_spec(tm, width):
    return pl.BlockSpec((tm, width), lambda i: (i, 0))


def _mod_specs(mod, layer, parts, prompt, tm):
    if prompt:
        arr = mod.reshape(2, N_COND, 1, 3 * D)
        specs = [pl.BlockSpec((None, None, 1, D),
                              lambda i, *_, p=p: (layer, N_SAMPLE + (i * tm) // SEQ, 0, p)) for p in parts]
    else:
        arr = mod
        specs = [pl.BlockSpec((None, N_SAMPLE, D), lambda i, *_, p=p: (layer, 0, p)) for p in parts]
    return [arr] * len(parts), specs


def _cond_kernel(c_ref, w_ref, b_ref, o_ref):
    a = _silu(c_ref[...]).astype(BF16)
    o_ref[...] = _dot(a, w_ref[...].astype(BF16)) + b_ref[...]


def _cond_proj(c, w, b):
    tn = 512
    return pl.pallas_call(
        _cond_kernel,
        grid=(2, 3 * D // tn),
        in_specs=[pl.BlockSpec((N_COND, D), lambda l, j: (0, 0)),
                  pl.BlockSpec((None, D, tn), lambda l, j: (l, 0, j)),
                  pl.BlockSpec((None, 1, tn), lambda l, j: (l, 0, j))],
        out_specs=pl.BlockSpec((None, N_COND, tn), lambda l, j: (l, 0, j)),
        out_shape=jax.ShapeDtypeStruct((2, N_COND, 3 * D), F32),
        compiler_params=_params(("arbitrary", "arbitrary")),
        name="cond_proj",
    )(c, w, b.reshape(2, 1, 3 * D))


def _group_layernorm(v, g, b):
    outs = []
    for k in range(SGU_GROUPS):
        sl = slice(k * GROUP_W, (k + 1) * GROUP_W)
        outs.append(_layernorm(v[:, sl], g[:, sl], b[:, sl]))
    return outs


def _mix0_in_kernel(x_ref, sh_ref, sc_ref, w_ref, lng_ref, lnb_ref, ws_ref, bst_ref,
                    xr_ref, ggr_ref, sgu_ref, *, tm):
    h = (x_ref[...] * (1.0 + sc_ref[...]) + sh_ref[...]).astype(BF16)
    xr_ref[...] = _dot(h, w_ref[:, 0:RNN_W])
    ggr_ref[...] = jax.nn.gelu(_dot(h, w_ref[:, RNN_W:2 * RNN_W]))
    gu = jax.nn.gelu(_dot(h, w_ref[:, 2 * RNN_W:2 * RNN_W + SGU_W]))
    gv = jax.nn.gelu(_dot(h, w_ref[:, 2 * RNN_W + SGU_W:]))
    vn = _group_layernorm(gv, lng_ref[...], lnb_ref[...])
    row = lax.broadcasted_iota(jnp.int32, (CHUNK, CHUNK), 0)
    col = lax.broadcasted_iota(jnp.int32, (CHUNK, CHUNK), 1)
    for k in range(SGU_GROUPS):
        ws = jnp.where(col <= row, ws_ref[k], 0.0).astype(BF16)
        vk = vn[k].astype(BF16)
        bias = bst_ref[:, k:k + 1]
        for c in range(tm // CHUNK):
            rows = slice(c * CHUNK, (c + 1) * CHUNK)
            cols = slice(k * GROUP_W, (k + 1) * GROUP_W)
            s = _dot(ws, vk[rows]) + bias
            sgu_ref[rows, cols] = gu[rows, cols] * s


def _mix0_in(x, mod, w_in, ln_g, ln_b, ws, bs_t, tm=512):
    t = x.shape[0]
    mod_args, mod_specs = _mod_specs(mod, 0, (0, 1), True, tm)
    out = jax.ShapeDtypeStruct((t, RNN_W), F32)
    return pl.pallas_call(
        functools.partial(_mix0_in_kernel, tm=tm),
        grid=(t // tm,),
        in_specs=[_row_spec(tm, D), *mod_specs, _resident((D, 4 * RNN_W)),
                  _resident((1, SGU_W)), _resident((1, SGU_W)),
                  _resident((SGU_GROUPS, CHUNK, CHUNK)), _resident((CHUNK, SGU_GROUPS))],
        out_specs=[_row_spec(tm, RNN_W)] * 3,
        out_shape=[out] * 3,
        compiler_params=_params(("arbitrary",)),
        name="mix0_in",
    )(x, *mod_args, w_in, ln_g, ln_b, ws, bs_t)


def _lru_gates(xc, wa_ref, ba, wx_ref, bx, lam):
    xb = xc.astype(BF16)
    lo, hi = xb[:, :RNN_HALF], xb[:, RNN_HALF:]
    r = jax.nn.sigmoid(jnp.concatenate([_dot(lo, wa_ref[0]), _dot(hi, wa_ref[1])], axis=1) + ba)
    i = jax.nn.sigmoid(jnp.concatenate([_dot(lo, wx_ref[0]), _dot(hi, wx_ref[1])], axis=1) + bx)
    log_a = (-LRU_C) * r * jax.nn.softplus(-lam)
    a = jnp.exp(log_a)
    u = jnp.sqrt(-jnp.tanh(log_a) * (a * a + 1.0)) * (i * xc)
    return a, u


def _mix0_rnn_kernel(xr_ref, ggr_ref, sgu_ref, x_ref, gt_ref, cw_ref, cb_ref, wa_ref, ba_ref,
                     wx_ref, bx_ref, lam_ref, wo_ref, g_ref, b_ref,
                     y_ref, ht_ref, buf_ref, a_ref, u_ref, hs_ref, hc_ref, *, ts):
    t = pl.program_id(1)

    @pl.when(t == 0)
    def _():
        buf_ref[0:8, :] = jnp.zeros((8, RNN_W), F32)
        hc_ref[...] = jnp.zeros((1, RNN_W), F32)

    xr = xr_ref[...]
    buf_ref[8:8 + ts, :] = xr
    cw = cw_ref[...]
    xc = (buf_ref[5:5 + ts, :] * cw[0:1] + buf_ref[6:6 + ts, :] * cw[1:2]
          + buf_ref[7:7 + ts, :] * cw[2:3] + xr * cw[3:4]) + cb_ref[...]
    buf_ref[0:8, :] = xr[ts - 8:ts, :]

    a, u = _lru_gates(xc, wa_ref, ba_ref[...], wx_ref, bx_ref[...], lam_ref[...])
    a_ref[...] = a
    u_ref[...] = u

    row = lax.broadcasted_iota(jnp.int32, (8, RNN_W), 0)

    def tile(k, h_prev):
        off = pl.multiple_of(k * 8, 8)
        aa = a_ref[pl.ds(off, 8), :]
        uu = u_ref[pl.ds(off, 8), :]
        for d in (1, 2, 4):
            keep = row >= d
            a_sh = jnp.where(keep, pltpu.roll(aa, d, 0), 1.0)
            u_sh = jnp.where(keep, pltpu.roll(uu, d, 0), 0.0)
            uu = uu + aa * u_sh
            aa = aa * a_sh
        h = aa * h_prev + uu
        hs_ref[pl.ds(off, 8), :] = h
        return h[7:8, :]

    h_last = lax.fori_loop(0, ts // 8, tile, hc_ref[...], unroll=4)
    hc_ref[...] = h_last

    @pl.when(t == pl.num_programs(1) - 1)
    def _():
        ht_ref[...] = h_last

    rnn = (hs_ref[...] * ggr_ref[...]).astype(BF16)
    out = _dot(rnn, wo_ref[0:RNN_W, :]) + _dot(sgu_ref[...].astype(BF16), wo_ref[RNN_W:, :])
    y = ALPHA * x_ref[...] + (1.0 + gt_ref[...]) * out
    y_ref[...] = _layernorm(y, g_ref[...], b_ref[...])


def _mix0_rnn(xr, ggr, sgu, x, mod, conv_w, conv_b, wa, ba, wx, bx, lam, w_out, ln_g, ln_b, ts=512):
    nt = SEQ // ts
    arr = mod.reshape(2, N_COND, 1, 3 * D)
    tile_spec = lambda w: pl.BlockSpec((ts, w), lambda b, t: (b * nt + t, 0))
    res = lambda shape: pl.BlockSpec(shape, lambda b, t: (0,) * len(shape), pipeline_mode=pl.Buffered(1))
    return pl.pallas_call(
        functools.partial(_mix0_rnn_kernel, ts=ts),
        grid=(N_PROMPT, nt),
        in_specs=[tile_spec(RNN_W), tile_spec(RNN_W), tile_spec(SGU_W), tile_spec(D),
                  pl.BlockSpec((None, None, 1, D), lambda b, t: (0, N_SAMPLE + b, 0, 2)),
                  res((CONV_W, RNN_W)), res((1, RNN_W)),
                  res((2, RNN_HALF, RNN_HALF)), res((1, RNN_W)),
                  res((2, RNN_HALF, RNN_HALF)), res((1, RNN_W)), res((1, RNN_W)),
                  res((D, D)), res((1, D)), res((1, D))],
        out_specs=[tile_spec(D), pl.BlockSpec((None, 1, RNN_W), lambda b, t: (b, 0, 0))],
        out_shape=[jax.ShapeDtypeStruct((N_PROMPT * SEQ, D), F32),
                   jax.ShapeDtypeStruct((N_PROMPT, 1, RNN_W), F32)],
        scratch_shapes=[pltpu.VMEM((ts + 8, RNN_W), F32), pltpu.VMEM((ts, RNN_W), F32),
                        pltpu.VMEM((ts, RNN_W), F32), pltpu.VMEM((ts, RNN_W), F32),
                        pltpu.VMEM((1, RNN_W), F32)],
        compiler_params=_params(("arbitrary", "arbitrary")),
        name="mix0_rnn",
    )(xr, ggr, sgu, x, arr, conv_w, conv_b, wa, ba, wx, bx, lam, w_out, ln_g, ln_b)


def _mix0_sample_kernel(x_ref, sh_ref, sc_ref, gt_ref, w_ref, st_ref, h0_ref, cw_ref, cb_ref,
                        wa_ref, ba_ref, wx_ref, bx_ref, lam_ref, lng_ref, lnb_ref, ws0_ref, bs0_ref,
                        wo_ref, g_ref, b_ref, y_ref, xr_ref, ht_ref, vn_ref):
    x = x_ref[...]
    h = (x * (1.0 + sc_ref[...]) + sh_ref[...]).astype(BF16)
    xr = _dot(h, w_ref[:, 0:RNN_W])
    ggr = jax.nn.gelu(_dot(h, w_ref[:, RNN_W:2 * RNN_W]))
    gu = jax.nn.gelu(_dot(h, w_ref[:, 2 * RNN_W:2 * RNN_W + SGU_W]))
    gv = jax.nn.gelu(_dot(h, w_ref[:, 2 * RNN_W + SGU_W:]))
    xr_ref[...] = xr
    cw = cw_ref[...]
    xc = (st_ref[0] * cw[0:1] + st_ref[1] * cw[1:2] + st_ref[2] * cw[2:3] + xr * cw[3:4]) + cb_ref[...]
    a, u = _lru_gates(xc, wa_ref, ba_ref[...], wx_ref, bx_ref[...], lam_ref[...])
    h_new = a * h0_ref[...] + u
    ht_ref[...] = h_new
    vn = jnp.concatenate(_group_layernorm(gv, lng_ref[...], lnb_ref[...]), axis=1)
    vn_ref[...] = vn
    sgu = gu * (ws0_ref[...] * vn + bs0_ref[...])
    rnn = (h_new * ggr).astype(BF16)
    out = _dot(rnn, wo_ref[0:RNN_W, :]) + _dot(sgu.astype(BF16), wo_ref[RNN_W:, :])
    y = ALPHA * x + (1.0 + gt_ref[...]) * out
    y_ref[...] = _layernorm(y, g_ref[...], b_ref[...])


def _mix0_sample(x, mod, w_in, state, h0, conv_w, conv_b, wa, ba, wx, bx, lam, lng, lnb, ws0, bs0,
                 w_out, ln_g, ln_b):
    mod_args, mod_specs = _mod_specs(mod, 0, (0, 1, 2), False, N_SAMPLE)
    full = lambda a: pl.BlockSpec(a.shape, lambda i: (0,) * a.ndim)
    rest = (w_in, state, h0, conv_w, conv_b, wa, ba, wx, bx, lam, lng, lnb, ws0, bs0, w_out, ln_g, ln_b)
    vec = jax.ShapeDtypeStruct((N_SAMPLE, RNN_W), F32)
    return pl.pallas_call(
        _mix0_sample_kernel,
        grid=(1,),
        in_specs=[full(x), *mod_specs, *[full(a) for a in rest]],
        out_specs=[_row_spec(N_SAMPLE, D)] + [_row_spec(N_SAMPLE, RNN_W)] * 3,
        out_shape=[jax.ShapeDtypeStruct((N_SAMPLE, D), F32), vec, vec, vec],
        compiler_params=_params(("arbitrary",)),
        name="mix0_sample",
    )(x, *mod_args, *rest)


def _ffn_kernel(x_ref, sh_ref, sc_ref, gt_ref, wg_ref, wu_ref, wd_ref, g_ref, b_ref, y_ref, acc_ref,
                *, bounds):
    x = x_ref[...]
    h = (x * (1.0 + sc_ref[...]) + sh_ref[...]).astype(BF16)
    for n, (lo, hi) in enumerate(bounds):
        act = (_silu(_dot(h, wg_ref[:, lo:hi])) * _dot(h, wu_ref[:, lo:hi])).astype(BF16)
        part = _dot(act, wd_ref[lo:hi, :])
        if n == 0:
            acc_ref[...] = part
        else:
            acc_ref[...] += part
    y = ALPHA * x + (1.0 + gt_ref[...]) * acc_ref[...]
    y_ref[...] = _layernorm(y, g_ref[...], b_ref[...])


def _ffn(x, mod, prompt, wg, wu, wd, ln_g, ln_b, tm):
    t = x.shape[0]
    step = 512
    bounds = tuple((lo, min(lo + step, D_FF)) for lo in range(0, D_FF, step))
    mod_args, mod_specs = _mod_specs(mod, 0, (0, 1, 2), prompt, tm)
    return pl.pallas_call(
        functools.partial(_ffn_kernel, bounds=bounds),
        grid=(t // tm,),
        in_specs=[_row_spec(tm, D), *mod_specs, _resident((D, D_FF)), _resident((D, D_FF)),
                  _resident((D_FF, D)), _resident((1, D)), _resident((1, D))],
        out_specs=_row_spec(tm, D),
        out_shape=jax.ShapeDtypeStruct((t, D), F32),
        scratch_shapes=[pltpu.VMEM((tm, D), F32)],
        compiler_params=_params(("arbitrary",)),
        name="ffn_prompt" if prompt else "ffn_sample",
    )(x, *mod_args, wg, wu, wd, ln_g, ln_b)


def _rope(y, cos, sin_signed):
    lane = lax.broadcasted_iota(jnp.int32, y.shape, 1)
    first_half = (lane % MAP_W) < (ROT // 2)
    partner = jnp.where(first_half, pltpu.roll(y, HEAD_W - ROT // 2, 1), pltpu.roll(y, ROT // 2, 1))
    return y * cos + partner * sin_signed


def _qkv_kernel(x_ref, sh_ref, sc_ref, w_ref, cos_ref, sin_ref, q_ref, k_ref, v_ref):
    h = (x_ref[...] * (1.0 + sc_ref[...]) + sh_ref[...]).astype(BF16)
    cos, sin = cos_ref[...], sin_ref[...]
    width = 4 * HEAD_W
    for out_ref, base, rotate in ((q_ref, 0, True), (k_ref, D, True), (v_ref, 2 * D, False)):
        for j in range(D // width):
            y = _dot(h, w_ref[:, base + j * width:base + (j + 1) * width])
            if rotate:
                y = jnp.concatenate([_rope(y[:, n * HEAD_W:(n + 1) * HEAD_W], cos, sin)
                                     for n in range(width // HEAD_W)], axis=1)
            out_ref[:, j * width:(j + 1) * width] = y


def _qkv(x, mod, prompt, w_qkv, cos, sin, tm):
    t = x.shape[0]
    mod_args, mod_specs = _mod_specs(mod, 1, (0, 1), prompt, tm)
    if prompt:
        tab_spec = pl.BlockSpec((tm, HEAD_W), lambda i: (i % (SEQ // tm), 0))
    else:
        tab_spec = pl.BlockSpec((1, HEAD_W), lambda i: (0, 0))
    out = jax.ShapeDtypeStruct((t, D), F32)
    return pl.pallas_call(
        _qkv_kernel,
        grid=(t // tm,),
        in_specs=[_row_spec(tm, D), *mod_specs, _resident((D, 3 * D)), tab_spec, tab_spec],
        out_specs=[_row_spec(tm, D)] * 3,
        out_shape=[out] * 3,
        compiler_params=_params(("arbitrary",)),
        name="qkv_prompt" if prompt else "qkv_sample",
    )(x, *mod_args, w_qkv, cos, sin)


def _rope_tables(pos):
    half = ROT // 2
    inv = ROPE_THETA ** (-(jnp.arange(half, dtype=F32) * 2.0 / ROT))
    ang = pos.astype(F32)[:, None] * inv
    ones = jnp.ones((pos.shape[0], MAP_W - ROT), F32)
    cos = jnp.concatenate([jnp.cos(ang), jnp.cos(ang), ones], axis=1)
    sin = jnp.concatenate([-jnp.sin(ang), jnp.sin(ang), 0.0 * ones], axis=1)
    return jnp.tile(cos, (1, 2)), jnp.tile(sin, (1, 2))


def _diff_lambda(lp):
    s1 = jnp.sum(lp[0:1] * lp[1:2], axis=-1, keepdims=True)
    s2 = jnp.sum(lp[2:3] * lp[3:4], axis=-1, keepdims=True)
    return jnp.exp(s1) - jnp.exp(s2) + LAM_INIT


def _flash_step(lp_ref, q_ref, k_ref, v_ref, o_ref, kb_ref, vb_ref, qs_ref, m_ref, acc_ref, qi, tq):
    @pl.when(qi == 0)
    def _():
        kb_ref[...] = k_ref[...].astype(BF16)
        vb_ref[:, 0:HEAD_W] = v_ref[...].astype(BF16)
        vb_ref[:, HEAD_W:] = jnp.ones((SEQ, HEAD_W), BF16)

    q = q_ref[...] * ATT_SCALE
    lane = lax.broadcasted_iota(jnp.int32, q.shape, 1)
    qs_ref[0] = jnp.where(lane < MAP_W, q, 0.0).astype(BF16)
    qs_ref[1] = jnp.where(lane >= MAP_W, q, 0.0).astype(BF16)
    m_ref[...] = jnp.full(m_ref.shape, NEG_INF, F32)
    acc_ref[...] = jnp.zeros(acc_ref.shape, F32)

    def block(j, diagonal):
        off = pl.multiple_of(j * tq, tq)
        kj = kb_ref[pl.ds(off, tq), :]
        vj = vb_ref[pl.ds(off, tq), :]
        if diagonal:
            row = lax.broadcasted_iota(jnp.int32, (tq, tq), 0)
            col = lax.broadcasted_iota(jnp.int32, (tq, tq), 1)
            visible = col <= row
        for m in range(2):
            s = _dot_nt(qs_ref[m], kj)
            if diagonal:
                s = jnp.where(visible, s, NEG_INF)
            m_prev = m_ref[m]
            m_new = jnp.maximum(m_prev, jnp.max(s, axis=-1, keepdims=True))
            p = jnp.exp(s - m_new).astype(BF16)
            acc_ref[m] = jnp.exp(m_prev - m_new) * acc_ref[m] + _dot(p, vj)
            m_ref[m] = m_new

    def below_diagonal(j, carry):
        block(j, False)
        return carry

    lax.fori_loop(0, qi, below_diagonal, 0)
    block(qi, True)
    o1 = acc_ref[0][:, 0:HEAD_W] / acc_ref[0][:, HEAD_W:]
    o2 = acc_ref[1][:, 0:HEAD_W] / acc_ref[1][:, HEAD_W:]
    o_ref[...] = o1 - _diff_lambda(lp_ref[...]) * o2


def _decode_step(lp_ref, q_ref, kn_ref, vn_ref, kc_refs, vc_refs, o_ref, qb_ref, m_ref, l_ref, acc_ref, first, last):
    @pl.when(first)
    def _():
        q = q_ref[...] * ATT_SCALE
        lane = lax.broadcasted_iota(jnp.int32, q.shape, 1)
        qb_ref[0:N_HEADS, :] = jnp.where(lane < MAP_W, q, 0.0)
        qb_ref[N_HEADS:, :] = jnp.where(lane >= MAP_W, q, 0.0)
        m_ref[...] = jnp.full(m_ref.shape, NEG_INF, F32)
        l_ref[...] = jnp.zeros(l_ref.shape, F32)
        acc_ref[...] = jnp.zeros(acc_ref.shape, F32)

    rows = PAGE * N_HEADS
    qb = qb_ref[...].astype(BF16)
    r = lax.broadcasted_iota(jnp.int32, (2 * N_HEADS, rows), 0)
    c = lax.broadcasted_iota(jnp.int32, (2 * N_HEADS, rows), 1)
    own_head = c % N_HEADS == r % N_HEADS
    scores = []
    for kc_ref in kc_refs:
        s = _dot_nt(qb, kc_ref[...].reshape(rows, HEAD_W).astype(BF16))
        scores.append(jnp.where(own_head, s, NEG_INF))
    m_prev = m_ref[...]
    m_new = m_prev
    for s in scores:
        m_new = jnp.maximum(m_new, jnp.max(s, axis=-1, keepdims=True))
    l_new = jnp.exp(m_prev - m_new) * l_ref[...]
    acc = jnp.exp(m_prev - m_new) * acc_ref[...]
    for s, vc_ref in zip(scores, vc_refs):
        p = jnp.exp(s - m_new)
        l_new = l_new + jnp.sum(p, axis=-1, keepdims=True)
        acc = acc + _dot(p.astype(BF16), vc_ref[...].reshape(rows, HEAD_W).astype(BF16))
    l_ref[...] = l_new
    acc_ref[...] = acc
    m_ref[...] = m_new

    @pl.when(last)
    def _():
        k_new = jnp.concatenate([kn_ref[...], kn_ref[...]], axis=0)
        v_new = jnp.concatenate([vn_ref[...], vn_ref[...]], axis=0)
        s_new = jnp.sum(qb_ref[...] * k_new, axis=-1, keepdims=True)
        m_old = m_ref[...]
        m_fin = jnp.maximum(m_old, s_new)
        a_old = jnp.exp(m_old - m_fin)
        p_new = jnp.exp(s_new - m_fin)
        l_fin = a_old * l_ref[...] + p_new
        o_maps = (a_old * acc_ref[...] + p_new * v_new) / l_fin
        o_ref[...] = o_maps[0:N_HEADS] - _diff_lambda(lp_ref[...]) * o_maps[N_HEADS:]


def _attn_kernel(pt_ref, lp_ref, q_ref, k_ref, v_ref, qs_ref, kn_ref, vn_ref, kc_ref, vc_ref, o_ref, os_ref,
                 kb_ref, vb_ref, qm_ref, m_ref, acc_ref, kpg_ref, vpg_ref, sem, qb_ref, ms_ref, ls_ref, accs_ref,
                 *, tq, n_pg, runs):
    nq = pl.num_programs(2)
    qi = pl.program_id(2)
    s = (pl.program_id(0) * pl.num_programs(1) + pl.program_id(1)) * nq + qi
    n_steps = pl.num_programs(0) * pl.num_programs(1) * nq
    slot = s % 2

    def page_copies(step, into):
        token, run = step // runs, step % runs
        for g in range(n_pg):
            page = pt_ref[token, run * n_pg + g]
            yield pltpu.make_async_copy(kc_ref.at[page], kpg_ref.at[into, g], sem.at[into, 0])
            yield pltpu.make_async_copy(vc_ref.at[page], vpg_ref.at[into, g], sem.at[into, 1])

    @pl.when(s == 0)
    def _():
        for copy in page_copies(s, slot):
            copy.start()

    @pl.when(s + 1 < n_steps)
    def _():
        for copy in page_copies(s + 1, 1 - slot):
            copy.start()

    _flash_step(lp_ref, q_ref, k_ref, v_ref, o_ref, kb_ref, vb_ref, qm_ref, m_ref, acc_ref, qi, tq)

    for copy in page_copies(s, slot):
        copy.wait()
    _decode_step(lp_ref, qs_ref, kn_ref, vn_ref,
                 [kpg_ref.at[slot, g] for g in range(n_pg)], [vpg_ref.at[slot, g] for g in range(n_pg)],
                 os_ref, qb_ref, ms_ref, ls_ref, accs_ref, s % runs == 0, s % runs == runs - 1)


def _attention(lp, q, k, v, q_s, k_s, v_s, cache_k, cache_v, page_table, tq=512):
    nq = SEQ // tq
    n_steps = N_PROMPT * N_HEADS * nq
    n_pages = page_table.shape[1]
    runs = n_steps // N_SAMPLE
    n_pg = n_pages // runs
    assert runs * N_SAMPLE == n_steps and n_pg * runs == n_pages
    q3, k3, v3 = (a.reshape(N_PROMPT, SEQ, D) for a in (q, k, v))
    heads = lambda a: a.reshape(N_SAMPLE, N_HEADS, HEAD_W)
    kc = cache_k.reshape(-1, PAGE, N_HEADS, HEAD_W)
    vc = cache_v.reshape(-1, PAGE, N_HEADS, HEAD_W)
    q_spec = pl.BlockSpec((None, tq, HEAD_W), lambda b, h, i, pt: (b, i, h))
    kv_spec = pl.BlockSpec((None, SEQ, HEAD_W), lambda b, h, i, pt: (b, 0, h))
    token_spec = pl.BlockSpec((None, N_HEADS, HEAD_W),
                              lambda b, h, i, pt: (((b * N_HEADS + h) * nq + i) // runs, 0, 0))
    any_spec = pl.BlockSpec(memory_space=pl.ANY)
    n_maps = 2 * N_HEADS
    page_buf = pltpu.VMEM((2, n_pg, PAGE, N_HEADS, HEAD_W), F32)
    o_p, o_s = pl.pallas_call(
        functools.partial(_attn_kernel, tq=tq, n_pg=n_pg, runs=runs),
        grid_spec=pltpu.PrefetchScalarGridSpec(
            num_scalar_prefetch=1,
            grid=(N_PROMPT, N_HEADS, nq),
            in_specs=[pl.BlockSpec((4, MAP_W), lambda b, h, i, pt: (0, 0)), q_spec, kv_spec, kv_spec,
                      token_spec, token_spec, token_spec, any_spec, any_spec],
            out_specs=[q_spec, token_spec],
            scratch_shapes=[pltpu.VMEM((SEQ, HEAD_W), BF16), pltpu.VMEM((SEQ, 2 * HEAD_W), BF16),
                            pltpu.VMEM((2, tq, HEAD_W), BF16), pltpu.VMEM((2, tq, 1), F32),
                            pltpu.VMEM((2, tq, 2 * HEAD_W), F32),
                            page_buf, page_buf, pltpu.SemaphoreType.DMA((2, 2)),
                            pltpu.VMEM((n_maps, HEAD_W), F32), pltpu.VMEM((n_maps, 1), F32),
                            pltpu.VMEM((n_maps, 1), F32), pltpu.VMEM((n_maps, HEAD_W), F32)]),
        out_shape=[jax.ShapeDtypeStruct((N_PROMPT, SEQ, D), F32),
                   jax.ShapeDtypeStruct((N_SAMPLE, N_HEADS, HEAD_W), F32)],
        compiler_params=_params(("arbitrary",) * 3),
        name="diff_attn",
    )(page_table, lp, q3, k3, v3, heads(q_s), heads(k_s), heads(v_s), kc, vc)
    return o_p.reshape(N_PROMPT * SEQ, D), o_s.reshape(N_SAMPLE, D)


def _attn_out_kernel(o_ref, x_ref, gt_ref, sg_ref, wo_ref, g_ref, b_ref, y_ref):
    o = o_ref[...]
    heads = []
    for n in range(N_HEADS):
        oh = o[:, n * HEAD_W:(n + 1) * HEAD_W]
        ms = jnp.mean(oh * oh, axis=-1, keepdims=True)
        heads.append((oh * lax.rsqrt(ms + LN_EPS) * sg_ref[...]) * (1.0 - LAM_INIT))
    out = _dot(jnp.concatenate(heads, axis=1).astype(BF16), wo_ref[...])
    y = ALPHA * x_ref[...] + (1.0 + gt_ref[...]) * out
    y_ref[...] = _layernorm(y, g_ref[...], b_ref[...])


def _attn_out(o, x, mod, prompt, subln_g, w_o, ln_g, ln_b, tm):
    t = x.shape[0]
    mod_args, mod_specs = _mod_specs(mod, 1, (2,), prompt, tm)
    return pl.pallas_call(
        _attn_out_kernel,
        grid=(t // tm,),
        in_specs=[_row_spec(tm, D), _row_spec(tm, D), *mod_specs, _resident((1, HEAD_W)),
                  _resident((D, D)), _resident((1, D)), _resident((1, D))],
        out_specs=_row_spec(tm, D),
        out_shape=jax.ShapeDtypeStruct((t, D), F32),
        compiler_params=_params(("arbitrary",)),
        name="attn_out_prompt" if prompt else "attn_out_sample",
    )(o, x, *mod_args, subln_g, w_o, ln_g, ln_b)


ROUTE_IDX = N_EXPERTS
ROUTE_GATE = N_EXPERTS + 2
MOE_TM = 1024
MOE_HALF = MOE_TM // 2
MOE_TF = 512
ROW_TILE = 256


def _route(logits):
    lane = lax.broadcasted_iota(jnp.int32, logits.shape, 1)
    z = jnp.where(lane < N_EXPERTS, logits, NEG_INF)
    m1 = jnp.max(z, axis=-1, keepdims=True)
    i1 = jnp.min(jnp.where(z == m1, lane, LANES), axis=-1, keepdims=True)
    z2 = jnp.where(lane == i1, NEG_INF, z)
    m2 = jnp.max(z2, axis=-1, keepdims=True)
    i2 = jnp.min(jnp.where(z2 == m2, lane, LANES), axis=-1, keepdims=True)
    e2 = jnp.exp(m2 - m1)
    g1 = 1.0 / (1.0 + e2)
    g2 = e2 / (1.0 + e2)
    out = jnp.where(lane == ROUTE_IDX, i1.astype(F32), 0.0) + jnp.where(lane == ROUTE_IDX + 1, i2.astype(F32), 0.0)
    return out + jnp.where(lane == ROUTE_GATE, g1, 0.0) + jnp.where(lane == ROUTE_GATE + 1, g2, 0.0)


def _router_kernel(x_ref, sh_ref, sc_ref, wr_ref, r_ref):
    h = (x_ref[...] * (1.0 + sc_ref[...]) + sh_ref[...]).astype(BF16)
    r_ref[...] = _route(_dot(h, wr_ref[...].astype(BF16)))


def _router(x, mod, prompt, w_router, tm):
    t = x.shape[0]
    arrs, specs = _mod_specs(mod, 1, (0, 1), prompt, tm)
    return pl.pallas_call(
        _router_kernel,
        grid=(t // tm,),
        in_specs=[_row_spec(tm, D), *specs, _resident((D, LANES))],
        out_specs=_row_spec(tm, LANES),
        out_shape=jax.ShapeDtypeStruct((t, LANES), F32),
        compiler_params=_params(("arbitrary",)),
        name="router_prompt" if prompt else "router_sample",
    )(x, *arrs, w_router)


def _moe_plan(expert_ids, tm, n_tiles):
    n_pairs = expert_ids.shape[0]
    n_blocks = -(-n_pairs // LANES)
    padded_ids = jnp.pad(expert_ids, (0, n_blocks * LANES - n_pairs), constant_values=N_EXPERTS)
    onehot = (padded_ids[:, None] == jnp.arange(N_EXPERTS, dtype=jnp.int32)[None, :]).astype(jnp.int32)
    blocks = onehot.reshape(n_blocks, LANES, N_EXPERTS)
    block_counts = jnp.sum(blocks, axis=1)
    before = jnp.cumsum(block_counts, axis=0) - block_counts
    rank = (jnp.cumsum(blocks, axis=1) - blocks + before[:, None, :]).reshape(-1, N_EXPERTS)[:n_pairs]
    onehot = onehot[:n_pairs]
    counts = jnp.sum(block_counts, axis=0)
    tiles_per = (counts + tm - 1) // tm
    tile_end = jnp.cumsum(tiles_per)
    tile_start = tile_end - tiles_per
    pos = jnp.sum(onehot * (tile_start * tm + rank), axis=1)
    n_used = tile_end[-1]
    tile = jnp.minimum(jnp.arange(n_tiles, dtype=jnp.int32), n_used - 1)
    expert = jnp.sum((tile[:, None] >= tile_end[None, :]).astype(jnp.int32), axis=1)
    rows = jnp.clip(counts[expert] - (tile - tile_start[expert]) * tm, 0, tm)
    return pos.astype(jnp.int32), tile, expert.astype(jnp.int32), rows.astype(jnp.int32), n_used.reshape(1)


def _token_rows(x):
    return x.reshape(x.shape[0], D // LANES, LANES)


def _wait_tile_of_rows(src_ref, dst_ref, sem, choices=2):
    for _ in range(choices):
        pltpu.make_async_copy(src_ref, dst_ref, sem).wait()


def _dispatch_kernel(pos_ref, x_ref, sh_ref, sc_ref, xg_in_ref, xg_ref, h_ref, sem, *, tr, n_tok):
    del xg_in_ref
    i = pl.program_id(0)
    n = pl.num_programs(0)
    slot = i % 2
    sized = xg_ref.at[pl.ds(0, tr)]

    @pl.when(i >= 2)
    def _():
        _wait_tile_of_rows(h_ref.at[slot], sized, sem.at[slot])

    h_ref[slot] = _token_rows(x_ref[...] * (1.0 + sc_ref[...]) + sh_ref[...])

    def send(r, carry):
        for k in range(2):
            dst = pos_ref[k * n_tok + i * tr + r]
            pltpu.make_async_copy(h_ref.at[slot, r], xg_ref.at[dst], sem.at[slot]).start()
        return carry

    lax.fori_loop(0, tr, send, 0, unroll=8)

    @pl.when(i == n - 1)
    def _():
        @pl.when(i >= 1)
        def _():
            _wait_tile_of_rows(h_ref.at[1 - slot], sized, sem.at[1 - slot])
        _wait_tile_of_rows(h_ref.at[slot], sized, sem.at[slot])


def _dispatch(x, mod, prompt, pos, xg, tr):
    t = x.shape[0]
    arrs, specs = _mod_specs(mod, 1, (0, 1), prompt, tr)
    any_spec = pl.BlockSpec(memory_space=pl.ANY)
    return pl.pallas_call(
        functools.partial(_dispatch_kernel, tr=tr, n_tok=t),
        grid_spec=pltpu.PrefetchScalarGridSpec(
            num_scalar_prefetch=1,
            grid=(t // tr,),
            in_specs=[pl.BlockSpec((tr, D), lambda i, pos: (i, 0)), *specs, any_spec],
            out_specs=any_spec,
            scratch_shapes=[pltpu.VMEM((2, tr, D // LANES, LANES), F32), pltpu.SemaphoreType.DMA((2,))]),
        out_shape=jax.ShapeDtypeStruct(xg.shape, F32),
        input_output_aliases={4: 0},
        compiler_params=_params(("arbitrary",)),
        name="moe_dispatch_prompt" if prompt else "moe_dispatch_sample",
    )(pos, x, *arrs, xg)


def _expert_kernel(tile_ref, exp_ref, rows_ref, used_ref, xg_ref, wg_ref, wu_ref, wd_ref, yg_ref, hb_ref, acc_ref):
    i = pl.program_id(0)
    f = pl.program_id(1)
    last_f = pl.num_programs(1) - 1
    live = i < used_ref[0]

    for half in range(MOE_TM // MOE_HALF):
        rows = slice(half * MOE_HALF, (half + 1) * MOE_HALF)
        occupied = live & (rows_ref[i] > half * MOE_HALF)

        @pl.when(occupied)
        def _():
            @pl.when(f == 0)
            def _():
                hb_ref[rows, :] = xg_ref[rows].reshape(MOE_HALF, D).astype(BF16)

            h = hb_ref[rows, :]
            act = (_silu(_dot(h, wg_ref[...].astype(BF16))) * _dot(h, wu_ref[...].astype(BF16))).astype(BF16)
            part = _dot(act, wd_ref[...].astype(BF16))

            @pl.when(f == 0)
            def _():
                acc_ref[rows, :] = part

            @pl.when(f > 0)
            def _():
                acc_ref[rows, :] += part

            @pl.when(f == last_f)
            def _():
                yg_ref[rows] = _token_rows(acc_ref[rows, :])

        @pl.when(jnp.logical_not(occupied) & (f == 0))
        def _():
            yg_ref[rows] = jnp.zeros((MOE_HALF, D // LANES, LANES), F32)


def _experts(xg, tile, expert, rows, n_used, wg, wu, wd):
    n_tiles = xg.shape[0] // MOE_TM
    nf = D_FF_E // MOE_TF

    def f_block(i, f, used):
        return jnp.where(i < used[0], f, nf - 1)

    return pl.pallas_call(
        _expert_kernel,
        grid_spec=pltpu.PrefetchScalarGridSpec(
            num_scalar_prefetch=4,
            grid=(n_tiles, nf),
            in_specs=[pl.BlockSpec((MOE_TM, D // LANES, LANES), lambda i, f, tile, exp, rows, used: (tile[i], 0, 0)),
                      pl.BlockSpec((None, D, MOE_TF),
                                   lambda i, f, tile, exp, rows, used: (exp[i], 0, f_block(i, f, used))),
                      pl.BlockSpec((None, D, MOE_TF),
                                   lambda i, f, tile, exp, rows, used: (exp[i], 0, f_block(i, f, used))),
                      pl.BlockSpec((None, MOE_TF, D),
                                   lambda i, f, tile, exp, rows, used: (exp[i], f_block(i, f, used), 0))],
            out_specs=pl.BlockSpec((MOE_TM, D // LANES, LANES), lambda i, f, tile, exp, rows, used: (i, 0, 0)),
            scratch_shapes=[pltpu.VMEM((MOE_TM, D), BF16), pltpu.VMEM((MOE_TM, D), F32)]),
        out_shape=jax.ShapeDtypeStruct(xg.shape, F32),
        compiler_params=_params(("arbitrary", "arbitrary")),
        name="moe_experts",
    )(tile, expert, rows, n_used, xg, wg, wu, wd)


def _combine_kernel(pos_ref, yg_ref, r_ref, x_ref, gt_ref, g_ref, b_ref, y_ref, buf_ref, sem, *, tr, n_tok):
    i = pl.program_id(0)
    n = pl.num_programs(0)
    slot = i % 2

    def fetch(step, into):
        def one(r, carry):
            for k in range(2):
                src = pos_ref[k * n_tok + step * tr + r]
                pltpu.make_async_copy(yg_ref.at[src], buf_ref.at[into, k, r], sem.at[into]).start()
            return carry
        lax.fori_loop(0, tr, one, 0, unroll=8)

    @pl.when(i == 0)
    def _():
        fetch(0, 0)

    @pl.when(i + 1 < n)
    def _():
        fetch(i + 1, 1 - slot)

    _wait_tile_of_rows(yg_ref.at[pl.ds(0, tr)], buf_ref.at[slot, 0], sem.at[slot])
    route = r_ref[...]
    first = buf_ref[slot, 0].reshape(tr, D)
    second = buf_ref[slot, 1].reshape(tr, D)
    moe = route[:, ROUTE_GATE:ROUTE_GATE + 1] * first + route[:, ROUTE_GATE + 1:ROUTE_GATE + 2] * second
    y = ALPHA * x_ref[...] + (1.0 + gt_ref[...]) * moe
    y_ref[...] = _layernorm(y, g_ref[...], b_ref[...])


def _combine(yg, pos, route, x, mod, prompt, ln_g, ln_b, tr):
    t = x.shape[0]
    arrs, specs = _mod_specs(mod, 1, (2,), prompt, tr)
    return pl.pallas_call(
        functools.partial(_combine_kernel, tr=tr, n_tok=t),
        grid_spec=pltpu.PrefetchScalarGridSpec(
            num_scalar_prefetch=1,
            grid=(t // tr,),
            in_specs=[pl.BlockSpec(memory_space=pl.ANY),
                      pl.BlockSpec((tr, LANES), lambda i, pos: (i, 0)),
                      pl.BlockSpec((tr, D), lambda i, pos: (i, 0)), *specs,
                      pl.BlockSpec((1, D), lambda i, pos: (0, 0)), pl.BlockSpec((1, D), lambda i, pos: (0, 0))],
            out_specs=pl.BlockSpec((tr, D), lambda i, pos: (i, 0)),
            scratch_shapes=[pltpu.VMEM((2, 2, tr, D // LANES, LANES), F32), pltpu.SemaphoreType.DMA((2,))]),
        out_shape=jax.ShapeDtypeStruct((t, D), F32),
        compiler_params=_params(("arbitrary",)),
        name="moe_combine_prompt" if prompt else "moe_combine_sample",
    )(pos, yg, route, x, *arrs, ln_g, ln_b)


def _moe(xp, xs, mod, w_router, wg, wu, wd, ln_g, ln_b):
    tp, ts = xp.shape[0], xs.shape[0]
    w_r = jnp.pad(w_router, ((0, 0), (0, LANES - N_EXPERTS)))
    route_p = _router(xp, mod, True, w_r, tm=512)
    route_s = _router(xs, mod, False, w_r, tm=ts)
    ids = lambda r: r[:, ROUTE_IDX:ROUTE_IDX + 2].astype(jnp.int32).T.reshape(-1)
    n_pairs = 2 * (tp + ts)
    n_tiles = (n_pairs + N_EXPERTS * (MOE_TM - 1)) // MOE_TM
    pos, tile, expert, rows, n_used = _moe_plan(jnp.concatenate([ids(route_p), ids(route_s)]), MOE_TM, n_tiles)
    pos_p, pos_s = pos[:2 * tp], pos[2 * tp:]
    xg = jnp.zeros((n_tiles * MOE_TM, D // LANES, LANES), F32)
    xg = _dispatch(xp, mod, True, pos_p, xg, ROW_TILE)
    xg = _dispatch(xs, mod, False, pos_s, xg, ts)
    yg = _experts(xg, tile, expert, rows, n_used, wg, wu, wd)
    yp = _combine(yg, pos_p, route_p, xp, mod, True, ln_g, ln_b, ROW_TILE)
    ys = _combine(yg, pos_s, route_s, xs, mod, False, ln_g, ln_b, ts)
    return yp, ys


def _block_diag(w):
    per = RNN_HALF // w.shape[-1]
    halves = [jax.scipy.linalg.block_diag(*[w[n * per + j] for j in range(per)]) for n in range(2)]
    return jnp.stack(halves).astype(BF16)


def kernel(x_prompt, x_sample, state_rglru_conv, state_rglru_h, cache_k, cache_v, page_table, c_prompt, c_sample, mix_mod_w, mix_mod_b, mix_ln_g, mix_ln_b, ffn_mod_w, ffn_mod_b, ffn_ln_g, ffn_ln_b, w_in0, conv_w, conv_b, lru_wa, lru_ba, lru_wx, lru_bx, lru_lambda, sgu_ln_g, sgu_ln_b, sgu_ws, sgu_bs, w_out0, ffn_w_gate, ffn_w_up, ffn_w_down, w_qkv, lambda_q1, lambda_k1, lambda_q2, lambda_k2, subln_g, w_o, w_router, moe_w_gate, moe_w_up, moe_w_down):
    past = page_table.shape[1] * PAGE
    xp = x_prompt.reshape(N_PROMPT * SEQ, D)
    xs = x_sample.reshape(N_SAMPLE, D)
    vec = lambda a: a.reshape(1, -1)

    cond = jnp.concatenate([c_sample, c_prompt], axis=0)
    mix_mod = _cond_proj(cond, mix_mod_w, mix_mod_b)
    ffn_mod = _cond_proj(cond, ffn_mod_w, ffn_mod_b)

    w_in = w_in0[0].astype(BF16)
    w_out = w_out0[0].astype(BF16)
    wa, wx = _block_diag(lru_wa[0]), _block_diag(lru_wx[0])
    lru = (conv_w[0], vec(conv_b[0]), wa, vec(lru_ba[0]), wx, vec(lru_bx[0]), vec(lru_lambda[0]))
    sgu_g, sgu_b = vec(sgu_ln_g[0]), vec(sgu_ln_b[0])
    ln0 = (vec(mix_ln_g[0]), vec(mix_ln_b[0]))

    xr_p, ggr_p, sgu_p = _mix0_in(xp, mix_mod, w_in, sgu_g, sgu_b, sgu_ws[0], sgu_bs[0].T)
    xp, ht_p = _mix0_rnn(xr_p, ggr_p, sgu_p, xp, mix_mod, *lru, w_out, *ln0)

    ws0 = vec(jnp.repeat(sgu_ws[0][:, 0, 0], GROUP_W))
    bs0 = vec(jnp.repeat(sgu_bs[0][:, 0], GROUP_W))
    state = jnp.swapaxes(state_rglru_conv[0], 0, 1)
    xs, xr_s, ht_s, vn_s = _mix0_sample(xs, mix_mod, w_in, state, state_rglru_h[0], *lru,
                                        sgu_g, sgu_b, ws0, bs0, w_out, *ln0)

    ffn_w = (ffn_w_gate[0].astype(BF16), ffn_w_up[0].astype(BF16), ffn_w_down[0].astype(BF16),
             vec(ffn_ln_g[0]), vec(ffn_ln_b[0]))
    xp = _ffn(xp, ffn_mod, True, *ffn_w, tm=512)
    xs = _ffn(xs, ffn_mod, False, *ffn_w, tm=N_SAMPLE)

    wqkv = w_qkv[0].astype(BF16)
    lp = jnp.stack([lambda_q1[0], lambda_k1[0], lambda_q2[0], lambda_k2[0]])
    cos_p, sin_p = _rope_tables(jnp.arange(SEQ))
    cos_s, sin_s = _rope_tables(jnp.full((1,), past))
    q_p, k_p, v_p = _qkv(xp, mix_mod, True, wqkv, cos_p, sin_p, tm=512)
    q_s, k_s, v_s = _qkv(xs, mix_mod, False, wqkv, cos_s, sin_s, tm=N_SAMPLE)
    o_p, o_s = _attention(lp, q_p, k_p, v_p, q_s, k_s, v_s, cache_k, cache_v, page_table)
    att_w = (vec(subln_g[0]), w_o[0].astype(BF16), vec(mix_ln_g[1]), vec(mix_ln_b[1]))
    xp = _attn_out(o_p, xp, mix_mod, True, *att_w, tm=512)
    xs = _attn_out(o_s, xs, mix_mod, False, *att_w, tm=N_SAMPLE)

    xp, xs = _moe(xp, xs, ffn_mod, w_router[0], moe_w_gate[0], moe_w_up[0], moe_w_down[0],
                  vec(ffn_ln_g[1]), vec(ffn_ln_b[1]))

    new_conv_s = jnp.concatenate([state_rglru_conv[0][:, 1:], xr_s[:, None]], axis=1)
    return (xp.reshape(N_PROMPT, SEQ, D), xs.reshape(N_SAMPLE, 1, D),
            xr_p.reshape(N_PROMPT, SEQ, RNN_W)[:, SEQ - (CONV_W - 1):][None],
            ht_p.reshape(1, N_PROMPT, RNN_W),
            new_conv_s[None], ht_s[None], vn_s.reshape(1, N_SAMPLE, 1, SGU_W),
            k_p.reshape(1, N_PROMPT, SEQ, N_HEADS, HEAD_W), v_p.reshape(1, N_PROMPT, SEQ, N_HEADS, HEAD_W),
            k_s.reshape(1, N_SAMPLE, 1, N_HEADS, HEAD_W), v_s.reshape(1, N_SAMPLE, 1, N_HEADS, HEAD_W))
```

```python
import functools
import math

import jax
import jax.numpy as jnp
from jax import lax
from jax.experimental import pallas as pl
from jax.experimental.pallas import tpu as pltpu

F32 = jnp.float32
BF16 = jnp.bfloat16

D = 1024
SEQ = 2048
N_PROMPT = 8
N_SAMPLE = 32
N_COND = N_PROMPT + N_SAMPLE
PAGE = 128
RNN_W = 512
RNN_HALF = 256
SGU_W = 512
SGU_GROUPS = 4
GROUP_W = 128
CHUNK = 128
CONV_W = 4
LRU_C = 8.0
N_HEADS = 8
HEAD_W = 128
MAP_W = 64
ROT = 16
ROPE_THETA = 500000.0
D_FF = 2816
N_EXPERTS = 8
D_FF_E = 3584
LN_EPS = 1e-5
ALPHA = (2 * 2) ** 0.25
LAM_INIT = 0.8 - 0.6 * math.exp(-0.3 * 1)
ATT_SCALE = MAP_W ** -0.5
LANES = 128
NEG_INF = float("-inf")

VMEM_LIMIT = 56 * 1024 * 1024


def _params(sem):
    return pltpu.CompilerParams(dimension_semantics=sem, vmem_limit_bytes=VMEM_LIMIT)


def _dot(a, b):
    return jnp.dot(a, b, preferred_element_type=F32)


def _dot_nt(a, b):
    return lax.dot_general(a, b, (((1,), (1,)), ((), ())), preferred_element_type=F32)


def _layernorm(y, g, b):
    mu = jnp.mean(y, axis=-1, keepdims=True)
    yc = y - mu
    var = jnp.mean(yc * yc, axis=-1, keepdims=True)
    return yc * lax.rsqrt(var + LN_EPS) * g + b


def _silu(x):
    return x * jax.nn.sigmoid(x)


def _resident(shape):
    n = len(shape)
    return pl.BlockSpec(shape, lambda *_: (0,) * n, pipeline_mode=pl.Buffered(1))


def _row_spec(tm, width):
    return pl.BlockSpec((tm, width), lambda i: (i, 0))


def _mod_specs(mod, layer, parts, prompt, tm):
    if prompt:
        arr = mod.reshape(2, N_COND, 1, 3 * D)
        specs = [pl.BlockSpec((None, None, 1, D),
                              lambda i, *_, p=p: (layer, N_SAMPLE + (i * tm) // SEQ, 0, p)) for p in parts]
    else:
        arr = mod
        specs = [pl.BlockSpec((None, N_SAMPLE, D), lambda i, *_, p=p: (layer, 0, p)) for p in parts]
    return [arr] * len(parts), specs


def _cond_kernel(c_ref, w_ref, b_ref, o_ref):
    a = _silu(c_ref[...]).astype(BF16)
    o_ref[...] = _dot(a, w_ref[...].astype(BF16)) + b_ref[...]


def _cond_proj(c, w, b):
    tn = 512
    return pl.pallas_call(
        _cond_kernel,
        grid=(2, 3 * D // tn),
        in_specs=[pl.BlockSpec((N_COND, D), lambda l, j: (0, 0)),
                  pl.BlockSpec((None, D, tn), lambda l, j: (l, 0, j)),
                  pl.BlockSpec((None, 1, tn), lambda l, j: (l, 0, j))],
        out_specs=pl.BlockSpec((None, N_COND, tn), lambda l, j: (l, 0, j)),
        out_shape=jax.ShapeDtypeStruct((2, N_COND, 3 * D), F32),
        compiler_params=_params(("arbitrary", "arbitrary")),
        name="cond_proj",
    )(c, w, b.reshape(2, 1, 3 * D))


def _group_layernorm(v, g, b):
    outs = []
    for k in range(SGU_GROUPS):
        sl = slice(k * GROUP_W, (k + 1) * GROUP_W)
        outs.append(_layernorm(v[:, sl], g[:, sl], b[:, sl]))
    return outs


def _mix0_in_kernel(x_ref, sh_ref, sc_ref, w_ref, lng_ref, lnb_ref, ws_ref, bst_ref,
                    xr_ref, ggr_ref, sgu_ref, *, tm):
    h = (x_ref[...] * (1.0 + sc_ref[...]) + sh_ref[...]).astype(BF16)
    xr_ref[...] = _dot(h, w_ref[:, 0:RNN_W])
    ggr_ref[...] = jax.nn.gelu(_dot(h, w_ref[:, RNN_W:2 * RNN_W]))
    gu = jax.nn.gelu(_dot(h, w_ref[:, 2 * RNN_W:2 * RNN_W + SGU_W]))
    gv = jax.nn.gelu(_dot(h, w_ref[:, 2 * RNN_W + SGU_W:]))
    vn = _group_layernorm(gv, lng_ref[...], lnb_ref[...])
    row = lax.broadcasted_iota(jnp.int32, (CHUNK, CHUNK), 0)
    col = lax.broadcasted_iota(jnp.int32, (CHUNK, CHUNK), 1)
    for k in range(SGU_GROUPS):
        ws = jnp.where(col <= row, ws_ref[k], 0.0).astype(BF16)
        vk = vn[k].astype(BF16)
        bias = bst_ref[:, k:k + 1]
        for c in range(tm // CHUNK):
            rows = slice(c * CHUNK, (c + 1) * CHUNK)
            cols = slice(k * GROUP_W, (k + 1) * GROUP_W)
            s = _dot(ws, vk[rows]) + bias
            sgu_ref[rows, cols] = gu[rows, cols] * s


def _mix0_in(x, mod, w_in, ln_g, ln_b, ws, bs_t, tm=512):
    t = x.shape[0]
    mod_args, mod_specs = _mod_specs(mod, 0, (0, 1), True, tm)
    out = jax.ShapeDtypeStruct((t, RNN_W), F32)
    return pl.pallas_call(
        functools.partial(_mix0_in_kernel, tm=tm),
        grid=(t // tm,),
        in_specs=[_row_spec(tm, D), *mod_specs, _resident((D, 4 * RNN_W)),
                  _resident((1, SGU_W)), _resident((1, SGU_W)),
                  _resident((SGU_GROUPS, CHUNK, CHUNK)), _resident((CHUNK, SGU_GROUPS))],
        out_specs=[_row_spec(tm, RNN_W)] * 3,
        out_shape=[out] * 3,
        compiler_params=_params(("arbitrary",)),
        name="mix0_in",
    )(x, *mod_args, w_in, ln_g, ln_b, ws, bs_t)


def _lru_gates(xc, wa_ref, ba, wx_ref, bx, lam):
    xb = xc.astype(BF16)
    lo, hi = xb[:, :RNN_HALF], xb[:, RNN_HALF:]
    r = jax.nn.sigmoid(jnp.concatenate([_dot(lo, wa_ref[0]), _dot(hi, wa_ref[1])], axis=1) + ba)
    i = jax.nn.sigmoid(jnp.concatenate([_dot(lo, wx_ref[0]), _dot(hi, wx_ref[1])], axis=1) + bx)
    log_a = (-LRU_C) * r * jax.nn.softplus(-lam)
    a = jnp.exp(log_a)
    u = jnp.sqrt(-jnp.tanh(log_a) * (a * a + 1.0)) * (i * xc)
    return a, u


def _mix0_rnn_kernel(xr_ref, ggr_ref, sgu_ref, x_ref, gt_ref, cw_ref, cb_ref, wa_ref, ba_ref,
                     wx_ref, bx_ref, lam_ref, wo_ref, g_ref, b_ref,
                     y_ref, ht_ref, buf_ref, a_ref, u_ref, hs_ref, hc_ref, *, ts):
    t = pl.program_id(1)

    @pl.when(t == 0)
    def _():
        buf_ref[0:8, :] = jnp.zeros((8, RNN_W), F32)
        hc_ref[...] = jnp.zeros((1, RNN_W), F32)

    xr = xr_ref[...]
    buf_ref[8:8 + ts, :] = xr
    cw = cw_ref[...]
    xc = (buf_ref[5:5 + ts, :] * cw[0:1] + buf_ref[6:6 + ts, :] * cw[1:2]
          + buf_ref[7:7 + ts, :] * cw[2:3] + xr * cw[3:4]) + cb_ref[...]
    buf_ref[0:8, :] = xr[ts - 8:ts, :]

    a, u = _lru_gates(xc, wa_ref, ba_ref[...], wx_ref, bx_ref[...], lam_ref[...])
    a_ref[...] = a
    u_ref[...] = u

    row = lax.broadcasted_iota(jnp.int32, (8, RNN_W), 0)

    def tile(k, h_prev):
        off = pl.multiple_of(k * 8, 8)
        aa = a_ref[pl.ds(off, 8), :]
        uu = u_ref[pl.ds(off, 8), :]
        for d in (1, 2, 4):
            keep = row >= d
            a_sh = jnp.where(keep, pltpu.roll(aa, d, 0), 1.0)
            u_sh = jnp.where(keep, pltpu.roll(uu, d, 0), 0.0)
            uu = uu + aa * u_sh
            aa = aa * a_sh
        h = aa * h_prev + uu
        hs_ref[pl.ds(off, 8), :] = h
        return h[7:8, :]

    h_last = lax.fori_loop(0, ts // 8, tile, hc_ref[...], unroll=4)
    hc_ref[...] = h_last

    @pl.when(t == pl.num_programs(1) - 1)
    def _():
        ht_ref[...] = h_last

    rnn = (hs_ref[...] * ggr_ref[...]).astype(BF16)
    out = _dot(rnn, wo_ref[0:RNN_W, :]) + _dot(sgu_ref[...].astype(BF16), wo_ref[RNN_W:, :])
    y = ALPHA * x_ref[...] + (1.0 + gt_ref[...]) * out
    y_ref[...] = _layernorm(y, g_ref[...], b_ref[...])


def _mix0_rnn(xr, ggr, sgu, x, mod, conv_w, conv_b, wa, ba, wx, bx, lam, w_out, ln_g, ln_b, ts=512):
    nt = SEQ // ts
    arr = mod.reshape(2, N_COND, 1, 3 * D)
    tile_spec = lambda w: pl.BlockSpec((ts, w), lambda b, t: (b * nt + t, 0))
    res = lambda shape: pl.BlockSpec(shape, lambda b, t: (0,) * len(shape), pipeline_mode=pl.Buffered(1))
    return pl.pallas_call(
        functools.partial(_mix0_rnn_kernel, ts=ts),
        grid=(N_PROMPT, nt),
        in_specs=[tile_spec(RNN_W), tile_spec(RNN_W), tile_spec(SGU_W), tile_spec(D),
                  pl.BlockSpec((None, None, 1, D), lambda b, t: (0, N_SAMPLE + b, 0, 2)),
                  res((CONV_W, RNN_W)), res((1, RNN_W)),
                  res((2, RNN_HALF, RNN_HALF)), res((1, RNN_W)),
                  res((2, RNN_HALF, RNN_HALF)), res((1, RNN_W)), res((1, RNN_W)),
                  res((D, D)), res((1, D)), res((1, D))],
        out_specs=[tile_spec(D), pl.BlockSpec((None, 1, RNN_W), lambda b, t: (b, 0, 0))],
        out_shape=[jax.ShapeDtypeStruct((N_PROMPT * SEQ, D), F32),
                   jax.ShapeDtypeStruct((N_PROMPT, 1, RNN_W), F32)],
        scratch_shapes=[pltpu.VMEM((ts + 8, RNN_W), F32), pltpu.VMEM((ts, RNN_W), F32),
                        pltpu.VMEM((ts, RNN_W), F32), pltpu.VMEM((ts, RNN_W), F32),
                        pltpu.VMEM((1, RNN_W), F32)],
        compiler_params=_params(("arbitrary", "arbitrary")),
        name="mix0_rnn",
    )(xr, ggr, sgu, x, arr, conv_w, conv_b, wa, ba, wx, bx, lam, w_out, ln_g, ln_b)


def _mix0_sample_kernel(x_ref, sh_ref, sc_ref, gt_ref, w_ref, st_ref, h0_ref, cw_ref, cb_ref,
                        wa_ref, ba_ref, wx_ref, bx_ref, lam_ref, lng_ref, lnb_ref, ws0_ref, bs0_ref,
                        wo_ref, g_ref, b_ref, y_ref, xr_ref, ht_ref, vn_ref):
    x = x_ref[...]
    h = (x * (1.0 + sc_ref[...]) + sh_ref[...]).astype(BF16)
    xr = _dot(h, w_ref[:, 0:RNN_W])
    ggr = jax.nn.gelu(_dot(h, w_ref[:, RNN_W:2 * RNN_W]))
    gu = jax.nn.gelu(_dot(h, w_ref[:, 2 * RNN_W:2 * RNN_W + SGU_W]))
    gv = jax.nn.gelu(_dot(h, w_ref[:, 2 * RNN_W + SGU_W:]))
    xr_ref[...] = xr
    cw = cw_ref[...]
    xc = (st_ref[0] * cw[0:1] + st_ref[1] * cw[1:2] + st_ref[2] * cw[2:3] + xr * cw[3:4]) + cb_ref[...]
    a, u = _lru_gates(xc, wa_ref, ba_ref[...], wx_ref, bx_ref[...], lam_ref[...])
    h_new = a * h0_ref[...] + u
    ht_ref[...] = h_new
    vn = jnp.concatenate(_group_layernorm(gv, lng_ref[...], lnb_ref[...]), axis=1)
    vn_ref[...] = vn
    sgu = gu * (ws0_ref[...] * vn + bs0_ref[...])
    rnn = (h_new * ggr).astype(BF16)
    out = _dot(rnn, wo_ref[0:RNN_W, :]) + _dot(sgu.astype(BF16), wo_ref[RNN_W:, :])
    y = ALPHA * x + (1.0 + gt_ref[...]) * out
    y_ref[...] = _layernorm(y, g_ref[...], b_ref[...])


def _mix0_sample(x, mod, w_in, state, h0, conv_w, conv_b, wa, ba, wx, bx, lam, lng, lnb, ws0, bs0,
                 w_out, ln_g, ln_b):
    mod_args, mod_specs = _mod_specs(mod, 0, (0, 1, 2), False, N_SAMPLE)
    full = lambda a: pl.BlockSpec(a.shape, lambda i: (0,) * a.ndim)
    rest = (w_in, state, h0, conv_w, conv_b, wa, ba, wx, bx, lam, lng, lnb, ws0, bs0, w_out, ln_g, ln_b)
    vec = jax.ShapeDtypeStruct((N_SAMPLE, RNN_W), F32)
    return pl.pallas_call(
        _mix0_sample_kernel,
        grid=(1,),
        in_specs=[full(x), *mod_specs, *[full(a) for a in rest]],
        out_specs=[_row_spec(N_SAMPLE, D)] + [_row_spec(N_SAMPLE, RNN_W)] * 3,
        out_shape=[jax.ShapeDtypeStruct((N_SAMPLE, D), F32), vec, vec, vec],
        compiler_params=_params(("arbitrary",)),
        name="mix0_sample",
    )(x, *mod_args, *rest)


def _ffn_kernel(x_ref, sh_ref, sc_ref, gt_ref, wg_ref, wu_ref, wd_ref, g_ref, b_ref, y_ref, acc_ref,
                *, bounds):
    x = x_ref[...]
    h = (x * (1.0 + sc_ref[...]) + sh_ref[...]).astype(BF16)
    for n, (lo, hi) in enumerate(bounds):
        act = (_silu(_dot(h, wg_ref[:, lo:hi])) * _dot(h, wu_ref[:, lo:hi])).astype(BF16)
        part = _dot(act, wd_ref[lo:hi, :])
        if n == 0:
            acc_ref[...] = part
        else:
            acc_ref[...] += part
    y = ALPHA * x + (1.0 + gt_ref[...]) * acc_ref[...]
    y_ref[...] = _layernorm(y, g_ref[...], b_ref[...])


def _ffn(x, mod, prompt, wg, wu, wd, ln_g, ln_b, tm):
    t = x.shape[0]
    step = 512
    bounds = tuple((lo, min(lo + step, D_FF)) for lo in range(0, D_FF, step))
    mod_args, mod_specs = _mod_specs(mod, 0, (0, 1, 2), prompt, tm)
    return pl.pallas_call(
        functools.partial(_ffn_kernel, bounds=bounds),
        grid=(t // tm,),
        in_specs=[_row_spec(tm, D), *mod_specs, _resident((D, D_FF)), _resident((D, D_FF)),
                  _resident((D_FF, D)), _resident((1, D)), _resident((1, D))],
        out_specs=_row_spec(tm, D),
        out_shape=jax.ShapeDtypeStruct((t, D), F32),
        scratch_shapes=[pltpu.VMEM((tm, D), F32)],
        compiler_params=_params(("arbitrary",)),
        name="ffn_prompt" if prompt else "ffn_sample",
    )(x, *mod_args, wg, wu, wd, ln_g, ln_b)


def _rope(y, cos, sin_signed):
    lane = lax.broadcasted_iota(jnp.int32, y.shape, 1)
    first_half = (lane % MAP_W) < (ROT // 2)
    partner = jnp.where(first_half, pltpu.roll(y, HEAD_W - ROT // 2, 1), pltpu.roll(y, ROT // 2, 1))
    return y * cos + partner * sin_signed


def _qkv_kernel(x_ref, sh_ref, sc_ref, w_ref, cos_ref, sin_ref, q_ref, k_ref, v_ref):
    h = (x_ref[...] * (1.0 + sc_ref[...]) + sh_ref[...]).astype(BF16)
    cos, sin = cos_ref[...], sin_ref[...]
    width = 4 * HEAD_W
    for out_ref, base, rotate in ((q_ref, 0, True), (k_ref, D, True), (v_ref, 2 * D, False)):
        for j in range(D // width):
            y = _dot(h, w_ref[:, base + j * width:base + (j + 1) * width])
            if rotate:
                y = jnp.concatenate([_rope(y[:, n * HEAD_W:(n + 1) * HEAD_W], cos, sin)
                                     for n in range(width // HEAD_W)], axis=1)
            out_ref[:, j * width:(j + 1) * width] = y


def _qkv(x, mod, prompt, w_qkv, cos, sin, tm):
    t = x.shape[0]
    mod_args, mod_specs = _mod_specs(mod, 1, (0, 1), prompt, tm)
    if prompt:
        tab_spec = pl.BlockSpec((tm, HEAD_W), lambda i: (i % (SEQ // tm), 0))
    else:
        tab_spec = pl.BlockSpec((1, HEAD_W), lambda i: (0, 0))
    out = jax.ShapeDtypeStruct((t, D), F32)
    return pl.pallas_call(
        _qkv_kernel,
        grid=(t // tm,),
        in_specs=[_row_spec(tm, D), *mod_specs, _resident((D, 3 * D)), tab_spec, tab_spec],
        out_specs=[_row_spec(tm, D)] * 3,
        out_shape=[out] * 3,
        compiler_params=_params(("arbitrary",)),
        name="qkv_prompt" if prompt else "qkv_sample",
    )(x, *mod_args, w_qkv, cos, sin)


def _rope_tables(pos):
    half = ROT // 2
    inv = ROPE_THETA ** (-(jnp.arange(half, dtype=F32) * 2.0 / ROT))
    ang = pos.astype(F32)[:, None] * inv
    ones = jnp.ones((pos.shape[0], MAP_W - ROT), F32)
    cos = jnp.concatenate([jnp.cos(ang), jnp.cos(ang), ones], axis=1)
    sin = jnp.concatenate([-jnp.sin(ang), jnp.sin(ang), 0.0 * ones], axis=1)
    return jnp.tile(cos, (1, 2)), jnp.tile(sin, (1, 2))


def _diff_lambda(lp):
    s1 = jnp.sum(lp[0:1] * lp[1:2], axis=-1, keepdims=True)
    s2 = jnp.sum(lp[2:3] * lp[3:4], axis=-1, keepdims=True)
    return jnp.exp(s1) - jnp.exp(s2) + LAM_INIT


def _flash_step(lp_ref, q_ref, k_ref, v_ref, o_ref, kb_ref, vb_ref, qs_ref, m_ref, acc_ref, qi, tq):
    @pl.when(qi == 0)
    def _():
        kb_ref[...] = k_ref[...].astype(BF16)
        vb_ref[0:HEAD_W, :] = v_ref[...].T.astype(BF16)
        vb_ref[HEAD_W:, :] = jnp.ones((HEAD_W, SEQ), BF16)

    qt = (q_ref[...] * ATT_SCALE).T
    feature = lax.broadcasted_iota(jnp.int32, qt.shape, 0)
    qs_ref[:, 0:tq] = jnp.where(feature < MAP_W, qt, 0.0).astype(BF16)
    qs_ref[:, tq:] = jnp.where(feature >= MAP_W, qt, 0.0).astype(BF16)
    m_ref[...] = jnp.full(m_ref.shape, NEG_INF, F32)
    acc_ref[...] = jnp.zeros(acc_ref.shape, F32)

    def block(j, diagonal):
        off = pl.multiple_of(j * tq, tq)
        s = _dot(kb_ref[pl.ds(off, tq), :], qs_ref[...])
        if diagonal:
            key = lax.broadcasted_iota(jnp.int32, s.shape, 0)
            query = lax.broadcasted_iota(jnp.int32, s.shape, 1) % tq
            s = jnp.where(key <= query, s, NEG_INF)
        m_prev = m_ref[...]
        m_new = jnp.maximum(m_prev, jnp.max(s, axis=0, keepdims=True))
        p = jnp.exp(s - m_new).astype(BF16)
        acc_ref[...] = jnp.exp(m_prev - m_new) * acc_ref[...] + _dot(vb_ref[:, pl.ds(off, tq)], p)
        m_ref[...] = m_new

    def below_diagonal(j, carry):
        block(j, False)
        return carry

    lax.fori_loop(0, qi, below_diagonal, 0)
    block(qi, True)
    o = acc_ref[0:HEAD_W, :] / acc_ref[HEAD_W:, :]
    o_ref[...] = (o[:, 0:tq] - _diff_lambda(lp_ref[...]) * o[:, tq:]).T


def _decode_step(lp_ref, q_ref, kn_ref, vn_ref, kc_refs, vc_refs, o_ref, qb_ref, m_ref, l_ref, acc_ref, first, last):
    @pl.when(first)
    def _():
        q = q_ref[...] * ATT_SCALE
        lane = lax.broadcasted_iota(jnp.int32, q.shape, 1)
        qb_ref[0:N_HEADS, :] = jnp.where(lane < MAP_W, q, 0.0)
        qb_ref[N_HEADS:, :] = jnp.where(lane >= MAP_W, q, 0.0)
        m_ref[...] = jnp.full(m_ref.shape, NEG_INF, F32)
        l_ref[...] = jnp.zeros(l_ref.shape, F32)
        acc_ref[...] = jnp.zeros(acc_ref.shape, F32)

    rows = PAGE * N_HEADS
    qb = qb_ref[...].astype(BF16)
    r = lax.broadcasted_iota(jnp.int32, (2 * N_HEADS, rows), 0)
    c = lax.broadcasted_iota(jnp.int32, (2 * N_HEADS, rows), 1)
    own_head = c % N_HEADS == r % N_HEADS
    scores = []
    for kc_ref in kc_refs:
        s = _dot_nt(qb, kc_ref[...].reshape(rows, HEAD_W).astype(BF16))
        scores.append(jnp.where(own_head, s, NEG_INF))
    m_prev = m_ref[...]
    m_new = m_prev
    for s in scores:
        m_new = jnp.maximum(m_new, jnp.max(s, axis=-1, keepdims=True))
    l_new = jnp.exp(m_prev - m_new) * l_ref[...]
    acc = jnp.exp(m_prev - m_new) * acc_ref[...]
    for s, vc_ref in zip(scores, vc_refs):
        p = jnp.exp(s - m_new)
        l_new = l_new + jnp.sum(p, axis=-1, keepdims=True)
        acc = acc + _dot(p.astype(BF16), vc_ref[...].reshape(rows, HEAD_W).astype(BF16))
    l_ref[...] = l_new
    acc_ref[...] = acc
    m_ref[...] = m_new

    @pl.when(last)
    def _():
        k_new = jnp.concatenate([kn_ref[...], kn_ref[...]], axis=0)
        v_new = jnp.concatenate([vn_ref[...], vn_ref[...]], axis=0)
        s_new = jnp.sum(qb_ref[...] * k_new, axis=-1, keepdims=True)
        m_old = m_ref[...]
        m_fin = jnp.maximum(m_old, s_new)
        a_old = jnp.exp(m_old - m_fin)
        p_new = jnp.exp(s_new - m_fin)
        l_fin = a_old * l_ref[...] + p_new
        o_maps = (a_old * acc_ref[...] + p_new * v_new) / l_fin
        o_ref[...] = o_maps[0:N_HEADS] - _diff_lambda(lp_ref[...]) * o_maps[N_HEADS:]


def _attn_kernel(pt_ref, lp_ref, q_ref, k_ref, v_ref, qs_ref, kn_ref, vn_ref, kc_ref, vc_ref, o_ref, os_ref,
                 kb_ref, vb_ref, qm_ref, m_ref, acc_ref, kpg_ref, vpg_ref, sem, qb_ref, ms_ref, ls_ref, accs_ref,
                 *, tq, n_pg, runs):
    nq = pl.num_programs(2)
    qi = pl.program_id(2)
    s = (pl.program_id(0) * pl.num_programs(1) + pl.program_id(1)) * nq + qi
    n_steps = pl.num_programs(0) * pl.num_programs(1) * nq
    slot = s % 2

    def page_copies(step, into):
        token, run = step // runs, step % runs
        for g in range(n_pg):
            page = pt_ref[token, run * n_pg + g]
            yield pltpu.make_async_copy(kc_ref.at[page], kpg_ref.at[into, g], sem.at[into, 0])
            yield pltpu.make_async_copy(vc_ref.at[page], vpg_ref.at[into, g], sem.at[into, 1])

    @pl.when(s == 0)
    def _():
        for copy in page_copies(s, slot):
            copy.start()

    @pl.when(s + 1 < n_steps)
    def _():
        for copy in page_copies(s + 1, 1 - slot):
            copy.start()

    _flash_step(lp_ref, q_ref, k_ref, v_ref, o_ref, kb_ref, vb_ref, qm_ref, m_ref, acc_ref, qi, tq)

    for copy in page_copies(s, slot):
        copy.wait()
    _decode_step(lp_ref, qs_ref, kn_ref, vn_ref,
                 [kpg_ref.at[slot, g] for g in range(n_pg)], [vpg_ref.at[slot, g] for g in range(n_pg)],
                 os_ref, qb_ref, ms_ref, ls_ref, accs_ref, s % runs == 0, s % runs == runs - 1)


def _attention(lp, q, k, v, q_s, k_s, v_s, cache_k, cache_v, page_table, tq=512):
    nq = SEQ // tq
    n_steps = N_PROMPT * N_HEADS * nq
    n_pages = page_table.shape[1]
    runs = n_steps // N_SAMPLE
    n_pg = n_pages // runs
    assert runs * N_SAMPLE == n_steps and n_pg * runs == n_pages
    q3, k3, v3 = (a.reshape(N_PROMPT, SEQ, D) for a in (q, k, v))
    heads = lambda a: a.reshape(N_SAMPLE, N_HEADS, HEAD_W)
    kc = cache_k.reshape(-1, PAGE, N_HEADS, HEAD_W)
    vc = cache_v.reshape(-1, PAGE, N_HEADS, HEAD_W)
    q_spec = pl.BlockSpec((None, tq, HEAD_W), lambda b, h, i, pt: (b, i, h))
    kv_spec = pl.BlockSpec((None, SEQ, HEAD_W), lambda b, h, i, pt: (b, 0, h))
    token_spec = pl.BlockSpec((None, N_HEADS, HEAD_W),
                              lambda b, h, i, pt: (((b * N_HEADS + h) * nq + i) // runs, 0, 0))
    any_spec = pl.BlockSpec(memory_space=pl.ANY)
    n_maps = 2 * N_HEADS
    page_buf = pltpu.VMEM((2, n_pg, PAGE, N_HEADS, HEAD_W), F32)
    o_p, o_s = pl.pallas_call(
        functools.partial(_attn_kernel, tq=tq, n_pg=n_pg, runs=runs),
        grid_spec=pltpu.PrefetchScalarGridSpec(
            num_scalar_prefetch=1,
            grid=(N_PROMPT, N_HEADS, nq),
            in_specs=[pl.BlockSpec((4, MAP_W), lambda b, h, i, pt: (0, 0)), q_spec, kv_spec, kv_spec,
                      token_spec, token_spec, token_spec, any_spec, any_spec],
            out_specs=[q_spec, token_spec],
            scratch_shapes=[pltpu.VMEM((SEQ, HEAD_W), BF16), pltpu.VMEM((2 * HEAD_W, SEQ), BF16),
                            pltpu.VMEM((HEAD_W, 2 * tq), BF16), pltpu.VMEM((1, 2 * tq), F32),
                            pltpu.VMEM((2 * HEAD_W, 2 * tq), F32),
                            page_buf, page_buf, pltpu.SemaphoreType.DMA((2, 2)),
                            pltpu.VMEM((n_maps, HEAD_W), F32), pltpu.VMEM((n_maps, 1), F32),
                            pltpu.VMEM((n_maps, 1), F32), pltpu.VMEM((n_maps, HEAD_W), F32)]),
        out_shape=[jax.ShapeDtypeStruct((N_PROMPT, SEQ, D), F32),
                   jax.ShapeDtypeStruct((N_SAMPLE, N_HEADS, HEAD_W), F32)],
        compiler_params=_params(("arbitrary",) * 3),
        name="diff_attn",
    )(page_table, lp, q3, k3, v3, heads(q_s), heads(k_s), heads(v_s), kc, vc)
    return o_p.reshape(N_PROMPT * SEQ, D), o_s.reshape(N_SAMPLE, D)


def _attn_out_kernel(o_ref, x_ref, gt_ref, sg_ref, wo_ref, g_ref, b_ref, y_ref):
    o = o_ref[...]
    heads = []
    for n in range(N_HEADS):
        oh = o[:, n * HEAD_W:(n + 1) * HEAD_W]
        ms = jnp.mean(oh * oh, axis=-1, keepdims=True)
        heads.append((oh * lax.rsqrt(ms + LN_EPS) * sg_ref[...]) * (1.0 - LAM_INIT))
    out = _dot(jnp.concatenate(heads, axis=1).astype(BF16), wo_ref[...])
    y = ALPHA * x_ref[...] + (1.0 + gt_ref[...]) * out
    y_ref[...] = _layernorm(y, g_ref[...], b_ref[...])


def _attn_out(o, x, mod, prompt, subln_g, w_o, ln_g, ln_b, tm):
    t = x.shape[0]
    mod_args, mod_specs = _mod_specs(mod, 1, (2,), prompt, tm)
    return pl.pallas_call(
        _attn_out_kernel,
        grid=(t // tm,),
        in_specs=[_row_spec(tm, D), _row_spec(tm, D), *mod_specs, _resident((1, HEAD_W)),
                  _resident((D, D)), _resident((1, D)), _resident((1, D))],
        out_specs=_row_spec(tm, D),
        out_shape=jax.ShapeDtypeStruct((t, D), F32),
        compiler_params=_params(("arbitrary",)),
        name="attn_out_prompt" if prompt else "attn_out_sample",
    )(o, x, *mod_args, subln_g, w_o, ln_g, ln_b)


ROUTE_IDX = N_EXPERTS
ROUTE_GATE = N_EXPERTS + 2
MOE_TM = 1024
MOE_HALF = MOE_TM // 2
MOE_TF = 512
ROW_TILE = 256


def _route(logits):
    lane = lax.broadcasted_iota(jnp.int32, logits.shape, 1)
    z = jnp.where(lane < N_EXPERTS, logits, NEG_INF)
    m1 = jnp.max(z, axis=-1, keepdims=True)
    i1 = jnp.min(jnp.where(z == m1, lane, LANES), axis=-1, keepdims=True)
    z2 = jnp.where(lane == i1, NEG_INF, z)
    m2 = jnp.max(z2, axis=-1, keepdims=True)
    i2 = jnp.min(jnp.where(z2 == m2, lane, LANES), axis=-1, keepdims=True)
    e2 = jnp.exp(m2 - m1)
    g1 = 1.0 / (1.0 + e2)
    g2 = e2 / (1.0 + e2)
    out = jnp.where(lane == ROUTE_IDX, i1.astype(F32), 0.0) + jnp.where(lane == ROUTE_IDX + 1, i2.astype(F32), 0.0)
    return out + jnp.where(lane == ROUTE_GATE, g1, 0.0) + jnp.where(lane == ROUTE_GATE + 1, g2, 0.0)


def _router_kernel(x_ref, sh_ref, sc_ref, wr_ref, r_ref):
    h = (x_ref[...] * (1.0 + sc_ref[...]) + sh_ref[...]).astype(BF16)
    r_ref[...] = _route(_dot(h, wr_ref[...].astype(BF16)))


def _router(x, mod, prompt, w_router, tm):
    t = x.shape[0]
    arrs, specs = _mod_specs(mod, 1, (0, 1), prompt, tm)
    return pl.pallas_call(
        _router_kernel,
        grid=(t // tm,),
        in_specs=[_row_spec(tm, D), *specs, _resident((D, LANES))],
        out_specs=_row_spec(tm, LANES),
        out_shape=jax.ShapeDtypeStruct((t, LANES), F32),
        compiler_params=_params(("arbitrary",)),
        name="router_prompt" if prompt else "router_sample",
    )(x, *arrs, w_router)


def _moe_plan(expert_ids, tm, n_tiles):
    n_pairs = expert_ids.shape[0]
    n_blocks = -(-n_pairs // LANES)
    padded_ids = jnp.pad(expert_ids, (0, n_blocks * LANES - n_pairs), constant_values=N_EXPERTS)
    onehot = (padded_ids[:, None] == jnp.arange(N_EXPERTS, dtype=jnp.int32)[None, :]).astype(jnp.int32)
    blocks = onehot.reshape(n_blocks, LANES, N_EXPERTS)
    block_counts = jnp.sum(blocks, axis=1)
    before = jnp.cumsum(block_counts, axis=0) - block_counts
    rank = (jnp.cumsum(blocks, axis=1) - blocks + before[:, None, :]).reshape(-1, N_EXPERTS)[:n_pairs]
    onehot = onehot[:n_pairs]
    counts = jnp.sum(block_counts, axis=0)
    tiles_per = (counts + tm - 1) // tm
    tile_end = jnp.cumsum(tiles_per)
    tile_start = tile_end - tiles_per
    pos = jnp.sum(onehot * (tile_start * tm + rank), axis=1)
    n_used = tile_end[-1]
    tile = jnp.minimum(jnp.arange(n_tiles, dtype=jnp.int32), n_used - 1)
    expert = jnp.sum((tile[:, None] >= tile_end[None, :]).astype(jnp.int32), axis=1)
    rows = jnp.clip(counts[expert] - (tile - tile_start[expert]) * tm, 0, tm)
    return pos.astype(jnp.int32), tile, expert.astype(jnp.int32), rows.astype(jnp.int32), n_used.reshape(1)


def _token_rows(x):
    return x.reshape(x.shape[0], D // LANES, LANES)


def _wait_tile_of_rows(src_ref, dst_ref, sem, choices=2):
    for _ in range(choices):
        pltpu.make_async_copy(src_ref, dst_ref, sem).wait()


def _dispatch_kernel(pos_ref, x_ref, sh_ref, sc_ref, xg_in_ref, xg_ref, h_ref, sem, *, tr, n_tok):
    del xg_in_ref
    i = pl.program_id(0)
    n = pl.num_programs(0)
    slot = i % 2
    sized = xg_ref.at[pl.ds(0, tr)]

    @pl.when(i >= 2)
    def _():
        _wait_tile_of_rows(h_ref.at[slot], sized, sem.at[slot])

    h_ref[slot] = _token_rows(x_ref[...] * (1.0 + sc_ref[...]) + sh_ref[...])

    def send(r, carry):
        for k in range(2):
            dst = pos_ref[k * n_tok + i * tr + r]
            pltpu.make_async_copy(h_ref.at[slot, r], xg_ref.at[dst], sem.at[slot]).start()
        return carry

    lax.fori_loop(0, tr, send, 0, unroll=8)

    @pl.when(i == n - 1)
    def _():
        @pl.when(i >= 1)
        def _():
            _wait_tile_of_rows(h_ref.at[1 - slot], sized, sem.at[1 - slot])
        _wait_tile_of_rows(h_ref.at[slot], sized, sem.at[slot])


def _dispatch(x, mod, prompt, pos, xg, tr):
    t = x.shape[0]
    arrs, specs = _mod_specs(mod, 1, (0, 1), prompt, tr)
    any_spec = pl.BlockSpec(memory_space=pl.ANY)
    return pl.pallas_call(
        functools.partial(_dispatch_kernel, tr=tr, n_tok=t),
        grid_spec=pltpu.PrefetchScalarGridSpec(
            num_scalar_prefetch=1,
            grid=(t // tr,),
            in_specs=[pl.BlockSpec((tr, D), lambda i, pos: (i, 0)), *specs, any_spec],
            out_specs=any_spec,
            scratch_shapes=[pltpu.VMEM((2, tr, D // LANES, LANES), F32), pltpu.SemaphoreType.DMA((2,))]),
        out_shape=jax.ShapeDtypeStruct(xg.shape, F32),
        input_output_aliases={4: 0},
        compiler_params=_params(("arbitrary",)),
        name="moe_dispatch_prompt" if prompt else "moe_dispatch_sample",
    )(pos, x, *arrs, xg)


def _expert_kernel(tile_ref, exp_ref, rows_ref, used_ref, xg_ref, wg_ref, wu_ref, wd_ref, yg_ref, hb_ref, acc_ref):
    i = pl.program_id(0)
    f = pl.program_id(1)
    last_f = pl.num_programs(1) - 1
    live = i < used_ref[0]

    for half in range(MOE_TM // MOE_HALF):
        rows = slice(half * MOE_HALF, (half + 1) * MOE_HALF)
        occupied = live & (rows_ref[i] > half * MOE_HALF)

        @pl.when(occupied)
        def _():
            @pl.when(f == 0)
            def _():
                hb_ref[rows, :] = xg_ref[rows].reshape(MOE_HALF, D).astype(BF16)

            h = hb_ref[rows, :]
            act = (_silu(_dot(h, wg_ref[...].astype(BF16))) * _dot(h, wu_ref[...].astype(BF16))).astype(BF16)
            part = _dot(act, wd_ref[...].astype(BF16))

            @pl.when(f == 0)
            def _():
                acc_ref[rows, :] = part

            @pl.when(f > 0)
            def _():
                acc_ref[rows, :] += part

            @pl.when(f == last_f)
            def _():
                yg_ref[rows] = _token_rows(acc_ref[rows, :])

        @pl.when(jnp.logical_not(occupied) & (f == 0))
        def _():
            yg_ref[rows] = jnp.zeros((MOE_HALF, D // LANES, LANES), F32)


def _experts(xg, tile, expert, rows, n_used, wg, wu, wd):
    n_tiles = xg.shape[0] // MOE_TM
    nf = D_FF_E // MOE_TF

    def f_block(i, f, used):
        return jnp.where(i < used[0], f, nf - 1)

    return pl.pallas_call(
        _expert_kernel,
        grid_spec=pltpu.PrefetchScalarGridSpec(
            num_scalar_prefetch=4,
            grid=(n_tiles, nf),
            in_specs=[pl.BlockSpec((MOE_TM, D // LANES, LANES), lambda i, f, tile, exp, rows, used: (tile[i], 0, 0)),
                      pl.BlockSpec((None, D, MOE_TF),
                                   lambda i, f, tile, exp, rows, used: (exp[i], 0, f_block(i, f, used))),
                      pl.BlockSpec((None, D, MOE_TF),
                                   lambda i, f, tile, exp, rows, used: (exp[i], 0, f_block(i, f, used))),
                      pl.BlockSpec((None, MOE_TF, D),
                                   lambda i, f, tile, exp, rows, used: (exp[i], f_block(i, f, used), 0))],
            out_specs=pl.BlockSpec((MOE_TM, D // LANES, LANES), lambda i, f, tile, exp, rows, used: (i, 0, 0)),
            scratch_shapes=[pltpu.VMEM((MOE_TM, D), BF16), pltpu.VMEM((MOE_TM, D), F32)]),
        out_shape=jax.ShapeDtypeStruct(xg.shape, F32),
        compiler_params=_params(("arbitrary", "arbitrary")),
        name="moe_experts",
    )(tile, expert, rows, n_used, xg, wg, wu, wd)


def _combine_kernel(pos_ref, yg_ref, r_ref, x_ref, gt_ref, g_ref, b_ref, y_ref, buf_ref, sem, *, tr, n_tok):
    i = pl.program_id(0)
    n = pl.num_programs(0)
    slot = i % 2

    def fetch(step, into):
        def one(r, carry):
            for k in range(2):
                src = pos_ref[k * n_tok + step * tr + r]
                pltpu.make_async_copy(yg_ref.at[src], buf_ref.at[into, k, r], sem.at[into]).start()
            return carry
        lax.fori_loop(0, tr, one, 0, unroll=8)

    @pl.when(i == 0)
    def _():
        fetch(0, 0)

    @pl.when(i + 1 < n)
    def _():
        fetch(i + 1, 1 - slot)

    _wait_tile_of_rows(yg_ref.at[pl.ds(0, tr)], buf_ref.at[slot, 0], sem.at[slot])
    route = r_ref[...]
    first = buf_ref[slot, 0].reshape(tr, D)
    second = buf_ref[slot, 1].reshape(tr, D)
    moe = route[:, ROUTE_GATE:ROUTE_GATE + 1] * first + route[:, ROUTE_GATE + 1:ROUTE_GATE + 2] * second
    y = ALPHA * x_ref[...] + (1.0 + gt_ref[...]) * moe
    y_ref[...] = _layernorm(y, g_ref[...], b_ref[...])


def _combine(yg, pos, route, x, mod, prompt, ln_g, ln_b, tr):
    t = x.shape[0]
    arrs, specs = _mod_specs(mod, 1, (2,), prompt, tr)
    return pl.pallas_call(
        functools.partial(_combine_kernel, tr=tr, n_tok=t),
        grid_spec=pltpu.PrefetchScalarGridSpec(
            num_scalar_prefetch=1,
            grid=(t // tr,),
            in_specs=[pl.BlockSpec(memory_space=pl.ANY),
                      pl.BlockSpec((tr, LANES), lambda i, pos: (i, 0)),
                      pl.BlockSpec((tr, D), lambda i, pos: (i, 0)), *specs,
                      pl.BlockSpec((1, D), lambda i, pos: (0, 0)), pl.BlockSpec((1, D), lambda i, pos: (0, 0))],
            out_specs=pl.BlockSpec((tr, D), lambda i, pos: (i, 0)),
            scratch_shapes=[pltpu.VMEM((2, 2, tr, D // LANES, LANES), F32), pltpu.SemaphoreType.DMA((2,))]),
        out_shape=jax.ShapeDtypeStruct((t, D), F32),
        compiler_params=_params(("arbitrary",)),
        name="moe_combine_prompt" if prompt else "moe_combine_sample",
    )(pos, yg, route, x, *arrs, ln_g, ln_b)


def _moe(xp, xs, mod, w_router, wg, wu, wd, ln_g, ln_b):
    tp, ts = xp.shape[0], xs.shape[0]
    w_r = jnp.pad(w_router, ((0, 0), (0, LANES - N_EXPERTS)))
    route_p = _router(xp, mod, True, w_r, tm=512)
    route_s = _router(xs, mod, False, w_r, tm=ts)
    ids = lambda r: r[:, ROUTE_IDX:ROUTE_IDX + 2].astype(jnp.int32).T.reshape(-1)
    n_pairs = 2 * (tp + ts)
    n_tiles = (n_pairs + N_EXPERTS * (MOE_TM - 1)) // MOE_TM
    pos, tile, expert, rows, n_used = _moe_plan(jnp.concatenate([ids(route_p), ids(route_s)]), MOE_TM, n_tiles)
    pos_p, pos_s = pos[:2 * tp], pos[2 * tp:]
    xg = jnp.zeros((n_tiles * MOE_TM, D // LANES, LANES), F32)
    xg = _dispatch(xp, mod, True, pos_p, xg, ROW_TILE)
    xg = _dispatch(xs, mod, False, pos_s, xg, ts)
    yg = _experts(xg, tile, expert, rows, n_used, wg, wu, wd)
    yp = _combine(yg, pos_p, route_p, xp, mod, True, ln_g, ln_b, ROW_TILE)
    ys = _combine(yg, pos_s, route_s, xs, mod, False, ln_g, ln_b, ts)
    return yp, ys


def _block_diag(w):
    per = RNN_HALF // w.shape[-1]
    halves = [jax.scipy.linalg.block_diag(*[w[n * per + j] for j in range(per)]) for n in range(2)]
    return jnp.stack(halves).astype(BF16)


def kernel(x_prompt, x_sample, state_rglru_conv, state_rglru_h, cache_k, cache_v, page_table, c_prompt, c_sample, mix_mod_w, mix_mod_b, mix_ln_g, mix_ln_b, ffn_mod_w, ffn_mod_b, ffn_ln_g, ffn_ln_b, w_in0, conv_w, conv_b, lru_wa, lru_ba, lru_wx, lru_bx, lru_lambda, sgu_ln_g, sgu_ln_b, sgu_ws, sgu_bs, w_out0, ffn_w_gate, ffn_w_up, ffn_w_down, w_qkv, lambda_q1, lambda_k1, lambda_q2, lambda_k2, subln_g, w_o, w_router, moe_w_gate, moe_w_up, moe_w_down):
    past = page_table.shape[1] * PAGE
    xp = x_prompt.reshape(N_PROMPT * SEQ, D)
    xs = x_sample.reshape(N_SAMPLE, D)
    vec = lambda a: a.reshape(1, -1)

    cond = jnp.concatenate([c_sample, c_prompt], axis=0)
    mix_mod = _cond_proj(cond, mix_mod_w, mix_mod_b)
    ffn_mod = _cond_proj(cond, ffn_mod_w, ffn_mod_b)

    w_in = w_in0[0].astype(BF16)
    w_out = w_out0[0].astype(BF16)
    wa, wx = _block_diag(lru_wa[0]), _block_diag(lru_wx[0])
    lru = (conv_w[0], vec(conv_b[0]), wa, vec(lru_ba[0]), wx, vec(lru_bx[0]), vec(lru_lambda[0]))
    sgu_g, sgu_b = vec(sgu_ln_g[0]), vec(sgu_ln_b[0])
    ln0 = (vec(mix_ln_g[0]), vec(mix_ln_b[0]))

    xr_p, ggr_p, sgu_p = _mix0_in(xp, mix_mod, w_in, sgu_g, sgu_b, sgu_ws[0], sgu_bs[0].T)
    xp, ht_p = _mix0_rnn(xr_p, ggr_p, sgu_p, xp, mix_mod, *lru, w_out, *ln0)

    ws0 = vec(jnp.repeat(sgu_ws[0][:, 0, 0], GROUP_W))
    bs0 = vec(jnp.repeat(sgu_bs[0][:, 0], GROUP_W))
    state = jnp.swapaxes(state_rglru_conv[0], 0, 1)
    xs, xr_s, ht_s, vn_s = _mix0_sample(xs, mix_mod, w_in, state, state_rglru_h[0], *lru,
                                        sgu_g, sgu_b, ws0, bs0, w_out, *ln0)

    ffn_w = (ffn_w_gate[0].astype(BF16), ffn_w_up[0].astype(BF16), ffn_w_down[0].astype(BF16),
             vec(ffn_ln_g[0]), vec(ffn_ln_b[0]))
    xp = _ffn(xp, ffn_mod, True, *ffn_w, tm=512)
    xs = _ffn(xs, ffn_mod, False, *ffn_w, tm=N_SAMPLE)

    wqkv = w_qkv[0].astype(BF16)
    lp = jnp.stack([lambda_q1[0], lambda_k1[0], lambda_q2[0], lambda_k2[0]])
    cos_p, sin_p = _rope_tables(jnp.arange(SEQ))
    cos_s, sin_s = _rope_tables(jnp.full((1,), past))
    q_p, k_p, v_p = _qkv(xp, mix_mod, True, wqkv, cos_p, sin_p, tm=512)
    q_s, k_s, v_s = _qkv(xs, mix_mod, False, wqkv, cos_s, sin_s, tm=N_SAMPLE)
    o_p, o_s = _attention(lp, q_p, k_p, v_p, q_s, k_s, v_s, cache_k, cache_v, page_table)
    att_w = (vec(subln_g[0]), w_o[0].astype(BF16), vec(mix_ln_g[1]), vec(mix_ln_b[1]))
    xp = _attn_out(o_p, xp, mix_mod, True, *att_w, tm=512)
    xs = _attn_out(o_s, xs, mix_mod, False, *att_w, tm=N_SAMPLE)

    xp, xs = _moe(xp, xs, ffn_mod, w_router[0], moe_w_gate[0], moe_w_up[0], moe_w_down[0],
                  vec(ffn_ln_g[1]), vec(ffn_ln_b[1]))

    new_conv_s = jnp.concatenate([state_rglru_conv[0][:, 1:], xr_s[:, None]], axis=1)
    return (xp.reshape(N_PROMPT, SEQ, D), xs.reshape(N_SAMPLE, 1, D),
            xr_p.reshape(N_PROMPT, SEQ, RNN_W)[:, SEQ - (CONV_W - 1):][None],
            ht_p.reshape(1, N_PROMPT, RNN_W),
            new_conv_s[None], ht_s[None], vn_s.reshape(1, N_SAMPLE, 1, SGU_W),
            k_p.reshape(1, N_PROMPT, SEQ, N_HEADS, HEAD_W), v_p.reshape(1, N_PROMPT, SEQ, N_HEADS, HEAD_W),
            k_s.reshape(1, N_SAMPLE, 1, N_HEADS, HEAD_W), v_s.reshape(1, N_SAMPLE, 1, N_HEADS, HEAD_W))
```
